```python
import math
import jax, jax.numpy as jnp
from jax import lax
import numpy as np

D_MODEL = 1024
BATCH = 8
SEQ = 4096
DEPTH = 4

N_MIXERS = 2
N_HEADS = 8
QK_HEAD_DIM = 128
V_HEAD_DIM = 128
KV_RANK = 256
IDX_HEADS = 8
IDX_HEAD_DIM = 64
TOPK_MAX = 256
Q_BLOCK = 128
A_IN = N_HEADS * QK_HEAD_DIM + KV_RANK + IDX_HEADS * IDX_HEAD_DIM + IDX_HEAD_DIM + IDX_HEADS
GMLP_HALF = 2 * D_MODEL
GMLP_GROUPS = 8
GMLP_GROUP_DIM = GMLP_HALF // GMLP_GROUPS
CHUNK = 128
D_FF = -(-8 * D_MODEL // (3 * 256)) * 256
N_MOD = 6
EPS = 1e-6
N_A = (DEPTH + 1) // 2
N_B = DEPTH // 2

kernel_name = 'hybrid_dsa_gmlp_swiglu_adaln'


def rmsnorm(x, g):
    xf = x.astype(jnp.float32)
    y = xf * lax.rsqrt(jnp.mean(xf * xf, axis=-1, keepdims=True) + EPS)
    return (y * g.astype(jnp.float32)).astype(x.dtype)


def layernorm(x, g, b):
    xf = x.astype(jnp.float32)
    mu = jnp.mean(xf, axis=-1, keepdims=True)
    xc = xf - mu
    var = jnp.mean(xc * xc, axis=-1, keepdims=True)
    return (xc * lax.rsqrt(var + EPS) * g.astype(jnp.float32) + b.astype(jnp.float32)).astype(x.dtype)


def modulate(h, shift, scale):
    return h * (1 + scale[:, None, :]) + shift[:, None, :]


def dsa_mixer(h, w_in, g_kv, g_kidx, b_kidx, w_uk, w_uv, w_o):
    B, T, _ = h.shape
    proj = h @ w_in
    o1 = N_HEADS * QK_HEAD_DIM
    o2 = o1 + KV_RANK
    o3 = o2 + IDX_HEADS * IDX_HEAD_DIM
    o4 = o3 + IDX_HEAD_DIM
    q = proj[..., :o1].reshape(B, T, N_HEADS, QK_HEAD_DIM)
    c_kv = rmsnorm(proj[..., o1:o2], g_kv)
    q_idx = proj[..., o2:o3].reshape(B, T, IDX_HEADS, IDX_HEAD_DIM)
    k_idx = layernorm(proj[..., o3:o4], g_kidx, b_kidx)
    w_idx = proj[..., o4:] * (IDX_HEADS ** -0.5 * IDX_HEAD_DIM ** -0.5)
    q_lat = jnp.einsum('bthd,hdr->bthr', q, w_uk) * (QK_HEAD_DIM ** -0.5)
    top_k = min(TOPK_MAX, T // 4)
    nb = T // Q_BLOCK
    key_pos = jnp.arange(T)

    def to_blocks(a):
        return a.reshape(B, nb, Q_BLOCK, *a.shape[2:]).swapaxes(0, 1)

    def block(args):
        qi_b, wi_b, ql_b, start = args
        q_pos = start + jnp.arange(Q_BLOCK)
        causal = key_pos[None, :] <= q_pos[:, None]
        logits = jnp.einsum('bqhd,bsd->bqhs', qi_b, k_idx).astype(jnp.float32)
        score = jnp.einsum('bqhs,bqh->bqs', jax.nn.relu(logits), wi_b.astype(jnp.float32))
        score = jnp.where(causal[None], score, -jnp.inf)
        _, idx = lax.top_k(score, top_k)
        valid = idx <= q_pos[None, :, None]
        c_sel = jax.vmap(lambda cb, ib: cb[ib])(c_kv, idx)
        att = jnp.einsum('bqhr,bqkr->bqhk', ql_b, c_sel).astype(jnp.float32)
        att = jnp.where(valid[:, :, None, :], att, -jnp.inf)
        p = jax.nn.softmax(att, axis=-1).astype(c_sel.dtype)
        return jnp.einsum('bqhk,bqkr->bqhr', p, c_sel)

    starts = jnp.arange(nb, dtype=jnp.int32) * Q_BLOCK
    o_lat = lax.map(block, (to_blocks(q_idx), to_blocks(w_idx), to_blocks(q_lat), starts))
    o_lat = o_lat.swapaxes(0, 1).reshape(B, T, N_HEADS, KV_RANK)
    o = jnp.einsum('bthr,hrd->bthd', o_lat, w_uv).reshape(B, T, N_HEADS * V_HEAD_DIM)
    return o @ w_o


def gmlp_mixer(h, w_in, ln_g, ln_b, w_s, b_s, w_out):
    B, T, _ = h.shape
    z = jax.nn.gelu(h @ w_in, approximate=False)
    u = z[..., :GMLP_HALF]
    v = layernorm(z[..., GMLP_HALF:], ln_g, ln_b)
    nc = T // CHUNK
    v = v.reshape(B, nc, CHUNK, GMLP_GROUPS, GMLP_GROUP_DIM)
    w_causal = w_s * jnp.tril(jnp.ones((CHUNK, CHUNK), dtype=w_s.dtype))
    mixed = jnp.einsum('gts,bcsge->bctge', w_causal, v) + b_s.T[:, :, None]
    y = u * mixed.reshape(B, T, GMLP_HALF)
    return y @ w_out


def swiglu(h, w_gate, w_up, w_down):
    return (jax.nn.silu(h @ w_gate) * (h @ w_up)) @ w_down


def setup_inputs(seed: int = 0) -> dict:
    key = jax.random.key(seed)
    ks = jax.random.split(key, 24)
    f32 = jnp.float32
    nrm = lambda k, shape, s: jax.random.normal(k, shape, f32) * s
    D = D_MODEL
    return {
        'x': nrm(ks[0], (BATCH, SEQ, D), 1.0),
        'c': nrm(ks[1], (BATCH, D), 1.0),
        'mod_w': nrm(ks[2], (DEPTH, D, N_MOD * D), 0.5 * D ** -0.5),
        'mod_b': nrm(ks[3], (DEPTH, N_MOD * D), 0.02),
        'norm_mix_g': 1.0 + nrm(ks[4], (DEPTH, D), 0.02),
        'norm_ffn_g': 1.0 + nrm(ks[5], (DEPTH, D), 0.02),
        'a_w_in': nrm(ks[6], (N_A, D, A_IN), D ** -0.5),
        'a_g_kv': 1.0 + nrm(ks[7], (N_A, KV_RANK), 0.02),
        'a_g_kidx': 1.0 + nrm(ks[8], (N_A, IDX_HEAD_DIM), 0.02),
        'a_b_kidx': nrm(ks[9], (N_A, IDX_HEAD_DIM), 0.02),
        'a_w_uk': nrm(ks[10], (N_A, N_HEADS, QK_HEAD_DIM, KV_RANK), QK_HEAD_DIM ** -0.5),
        'a_w_uv': nrm(ks[11], (N_A, N_HEADS, KV_RANK, V_HEAD_DIM), KV_RANK ** -0.5),
        'a_w_o': nrm(ks[12], (N_A, N_HEADS * V_HEAD_DIM, D), (N_HEADS * V_HEAD_DIM) ** -0.5),
        'b_w_in': nrm(ks[13], (N_B, D, 2 * GMLP_HALF), D ** -0.5),
        'b_ln_g': 1.0 + nrm(ks[14], (N_B, GMLP_HALF), 0.02),
        'b_ln_b': nrm(ks[15], (N_B, GMLP_HALF), 0.02),
        'b_w_s': nrm(ks[16], (N_B, GMLP_GROUPS, CHUNK, CHUNK), 0.5 * CHUNK ** -0.5),
        'b_b_s': 1.0 + nrm(ks[17], (N_B, GMLP_GROUPS, CHUNK), 0.1),
        'b_w_out': nrm(ks[18], (N_B, GMLP_HALF, D), GMLP_HALF ** -0.5),
        'ffn_w_gate': nrm(ks[19], (DEPTH, D, D_FF), D ** -0.5),
        'ffn_w_up': nrm(ks[20], (DEPTH, D, D_FF), D ** -0.5),
        'ffn_w_down': nrm(ks[21], (DEPTH, D_FF, D), D_FF ** -0.5),
        'final_g': 1.0 + nrm(ks[22], (D,), 0.02),
    }


def reference(x, c, mod_w, mod_b, norm_mix_g, norm_ffn_g, a_w_in, a_g_kv, a_g_kidx, a_b_kidx,
              a_w_uk, a_w_uv, a_w_o, b_w_in, b_ln_g, b_ln_b, b_w_s, b_b_s, b_w_out,
              ffn_w_gate, ffn_w_up, ffn_w_down, final_g):
    c_act = jax.nn.silu(c)
    for layer in range(DEPTH):
        mod = c_act @ mod_w[layer] + mod_b[layer]
        sh1, sc1, g1, sh2, sc2, g2 = jnp.split(mod, N_MOD, axis=-1)
        h = modulate(rmsnorm(x, norm_mix_g[layer]), sh1, sc1)
        j = layer // N_MIXERS
        if layer % N_MIXERS == 0:
            y = dsa_mixer(h, a_w_in[j], a_g_kv[j], a_g_kidx[j], a_b_kidx[j],
                          a_w_uk[j], a_w_uv[j], a_w_o[j])
        else:
            y = gmlp_mixer(h, b_w_in[j], b_ln_g[j], b_ln_b[j], b_w_s[j], b_b_s[j], b_w_out[j])
        x = x + g1[:, None, :] * y
        h = modulate(rmsnorm(x, norm_ffn_g[layer]), sh2, sc2)
        x = x + g2[:, None, :] * swiglu(h, ffn_w_gate[layer], ffn_w_up[layer], ffn_w_down[layer])
    return rmsnorm(x, final_g)
```

```python
import functools

import jax
import jax.numpy as jnp
from jax import lax
from jax.experimental import pallas as pl
from jax.experimental.pallas import tpu as pltpu

F32 = jnp.float32
BF16 = jnp.bfloat16
I32 = jnp.int32

EPS = 1e-6
N_MOD = 6
N_HEADS = 8
QK_HEAD_DIM = 128
V_HEAD_DIM = 128
KV_RANK = 256
IDX_HEADS = 8
IDX_HEAD_DIM = 64
TOPK_MAX = 256
GMLP_GROUPS = 8
CHUNK = 128

TOKEN_TILE = 256
ATTN_Q_TILE = 256
ATTN_KEY_CHUNK = 512
VMEM_LIMIT_BYTES = 56 * 1024 * 1024

INT_MIN = -(2 ** 31)
NEG_BIG = -1e30

_NT_DIMS = (((1,), (1,)), ((), ()))


def _dot(a, b):
    return jnp.dot(a, b, preferred_element_type=F32)


def _dot_nt(a, b):
    return lax.dot_general(a, b, _NT_DIMS, preferred_element_type=F32)


def _norm_mod(x, g, shift, scale):
    ms = jnp.mean(x * x, axis=-1, keepdims=True)
    y = x * lax.rsqrt(ms + EPS) * g
    return y * (1.0 + scale) + shift


def _const_spec(block_shape, index):
    return pl.BlockSpec(block_shape, lambda *_: index, pipeline_mode=pl.Buffered(1))


def _params(n_axes):
    return pltpu.CompilerParams(dimension_semantics=("arbitrary",) * n_axes,
                                vmem_limit_bytes=VMEM_LIMIT_BYTES)


def _mod_kernel(c_ref, w_ref, b_ref, o_ref):
    c = c_ref[...]
    c_act = c * jax.nn.sigmoid(c)
    o_ref[0] = jnp.dot(c_act, w_ref[0], preferred_element_type=F32,
                       precision=lax.Precision.HIGHEST) + b_ref[0]


def _mod_call(c, mod_w, mod_b):
    depth, d, n = mod_w.shape
    b = c.shape[0]
    tn = 1024
    out = pl.pallas_call(
        _mod_kernel,
        grid=(depth, n // tn),
        in_specs=[pl.BlockSpec((b, d), lambda l, j: (0, 0)),
                  pl.BlockSpec((1, d, tn), lambda l, j: (l, 0, j)),
                  pl.BlockSpec((1, 1, tn), lambda l, j: (l, 0, j))],
        out_specs=pl.BlockSpec((1, b, tn), lambda l, j: (l, 0, j)),
        out_shape=jax.ShapeDtypeStruct((depth, b, n), F32),
        compiler_params=_params(2),
        name="mod",
    )(c, mod_w, mod_b.reshape(depth, 1, n))
    return out.reshape(depth, b, 1, n)


def _mod_spec(layer, which, d):
    return pl.BlockSpec((1, 1, 1, d), lambda b, t: (layer, b, 0, which))


def _ffn_kernel(x_ref, g_ref, sh_ref, sc_ref, gt_ref, wg_ref, wu_ref, wd_ref, *rest, final):
    o_ref = rest[-1]
    x = x_ref[0]
    h = _norm_mod(x, g_ref[...], sh_ref[0, 0], sc_ref[0, 0]).astype(BF16)
    gate = _dot(h, wg_ref[...])
    up = _dot(h, wu_ref[...])
    act = (gate * jax.nn.sigmoid(gate) * up).astype(BF16)
    y = _dot(act, wd_ref[...])
    xn = x + gt_ref[0, 0] * y
    if final:
        fg_ref = rest[0]
        ms = jnp.mean(xn * xn, axis=-1, keepdims=True)
        xn = xn * lax.rsqrt(ms + EPS) * fg_ref[...]
    o_ref[0] = xn


def _ffn_call(x, mod, layer, norm_g, wg, wu, wd, final_g):
    b, t, d = x.shape
    dff = wg.shape[-1]
    tm = TOKEN_TILE
    final = final_g is not None
    in_specs = [pl.BlockSpec((1, tm, d), lambda i, j: (i, j, 0)),
                _const_spec((1, d), (0, 0)),
                _mod_spec(layer, 3, d), _mod_spec(layer, 4, d), _mod_spec(layer, 5, d),
                _const_spec((None, d, dff), (layer, 0, 0)),
                _const_spec((None, d, dff), (layer, 0, 0)),
                _const_spec((None, dff, d), (layer, 0, 0))]
    args = [x, norm_g, mod, mod, mod, wg, wu, wd]
    if final:
        in_specs.append(_const_spec((1, d), (0, 0)))
        args.append(final_g)
    return pl.pallas_call(
        functools.partial(_ffn_kernel, final=final),
        grid=(b, t // tm),
        in_specs=in_specs,
        out_specs=pl.BlockSpec((1, tm, d), lambda i, j: (i, j, 0)),
        out_shape=jax.ShapeDtypeStruct((b, t, d), F32),
        compiler_params=_params(2),
        name="ffn",
    )(*args)


def _gmlp_kernel(x_ref, g_ref, sh_ref, sc_ref, gt_ref, win_ref, lng_ref, lnb_ref, ws_ref, bs_ref,
                 wout_ref, o_ref, y_scr):
    tm = x_ref.shape[1]
    half = lng_ref.shape[-1]
    gdim = half // GMLP_GROUPS
    x = x_ref[0]
    h = _norm_mod(x, g_ref[...], sh_ref[0, 0], sc_ref[0, 0]).astype(BF16)
    z = _dot(h, win_ref[...])
    z = 0.5 * z * (1.0 + lax.erf(z * (2.0 ** -0.5)))
    v = z[:, half:]
    mu = jnp.mean(v, axis=-1, keepdims=True)
    vc = v - mu
    var = jnp.mean(vc * vc, axis=-1, keepdims=True)
    vn = (vc * lax.rsqrt(var + EPS) * lng_ref[...] + lnb_ref[...]).astype(BF16)
    row = lax.broadcasted_iota(I32, (CHUNK, CHUNK), 0)
    col = lax.broadcasted_iota(I32, (CHUNK, CHUNK), 1)
    tril = col <= row
    for g in range(GMLP_GROUPS):
        w_causal = jnp.where(tril, ws_ref[g], 0.0).astype(BF16)
        bias = bs_ref[g]
        lo = g * gdim
        for c in range(tm // CHUNK):
            r0 = c * CHUNK
            mixed = _dot(w_causal, vn[r0:r0 + CHUNK, lo:lo + gdim]) + bias
            y_scr[r0:r0 + CHUNK, lo:lo + gdim] = (z[r0:r0 + CHUNK, lo:lo + gdim] * mixed).astype(BF16)
    y = _dot(y_scr[...], wout_ref[...])
    o_ref[0] = x + gt_ref[0, 0] * y


def _gmlp_call(x, mod, layer, norm_g, j, w_in, ln_g, ln_b, w_s, b_s, w_out):
    b, t, d = x.shape
    half = ln_g.shape[-1]
    tm = TOKEN_TILE
    return pl.pallas_call(
        _gmlp_kernel,
        grid=(b, t // tm),
        in_specs=[pl.BlockSpec((1, tm, d), lambda i, k: (i, k, 0)),
                  _const_spec((1, d), (0, 0)),
                  _mod_spec(layer, 0, d), _mod_spec(layer, 1, d), _mod_spec(layer, 2, d),
                  _const_spec((None, d, 2 * half), (j, 0, 0)),
                  _const_spec((None, 1, half), (j, 0, 0)),
                  _const_spec((None, 1, half), (j, 0, 0)),
                  _const_spec((None, GMLP_GROUPS, CHUNK, CHUNK), (j, 0, 0, 0)),
                  _const_spec((None, GMLP_GROUPS, CHUNK, 1), (j, 0, 0, 0)),
                  _const_spec((None, half, d), (j, 0, 0))],
        out_specs=pl.BlockSpec((1, tm, d), lambda i, k: (i, k, 0)),
        out_shape=jax.ShapeDtypeStruct((b, t, d), F32),
        scratch_shapes=[pltpu.VMEM((tm, half), BF16)],
        compiler_params=_params(2),
        name="gmlp",
    )(x, norm_g, mod, mod, mod, w_in, ln_g, ln_b, w_s, b_s, w_out)


_O_CKV = N_HEADS * QK_HEAD_DIM
_O_KIDX = _O_CKV + KV_RANK
_WA_COLS = _O_KIDX + 128
_QIDX_ROWS = IDX_HEADS * IDX_HEAD_DIM
_WB_ROWS = _QIDX_ROWS + 16


def _dsa_in_kernel(x_ref, g_ref, sh_ref, sc_ref, wa_ref, wbt_ref, wukt_ref, gkv_ref, gk_ref, bk_ref,
                   qlt_ref, ckv_ref, ckvt_ref, kidx_ref, qidxt_ref, widxt_ref):
    x = x_ref[0]
    h = _norm_mod(x, g_ref[...], sh_ref[0, 0], sc_ref[0, 0]).astype(BF16)
    pa = _dot(h, wa_ref[...])
    for hh in range(N_HEADS):
        qh = pa[:, hh * QK_HEAD_DIM:(hh + 1) * QK_HEAD_DIM].astype(BF16)
        qlt = _dot_nt(wukt_ref[hh], qh) * (QK_HEAD_DIM ** -0.5)
        qlt_ref[0, hh] = qlt.astype(BF16)
    ck = pa[:, _O_CKV:_O_KIDX]
    ckn = ck * lax.rsqrt(jnp.mean(ck * ck, axis=-1, keepdims=True) + EPS) * gkv_ref[...]
    ckv_ref[0] = ckn.astype(BF16)
    ckvt_ref[0, 0] = ckn.T.astype(BF16)
    kr = pa[:, _O_KIDX:_O_KIDX + 128]
    lane = lax.broadcasted_iota(I32, kr.shape, 1)
    real = lane < IDX_HEAD_DIM
    mu = jnp.sum(jnp.where(real, kr, 0.0), axis=-1, keepdims=True) * (1.0 / IDX_HEAD_DIM)
    kc = jnp.where(real, kr - mu, 0.0)
    var = jnp.sum(kc * kc, axis=-1, keepdims=True) * (1.0 / IDX_HEAD_DIM)
    kn = kc * lax.rsqrt(var + EPS) * gk_ref[...] + bk_ref[...]
    kidx_ref[0] = kn[:, :IDX_HEAD_DIM].astype(BF16)
    pb = _dot_nt(wbt_ref[...], h)
    qidxt_ref[0] = pb[:_QIDX_ROWS].astype(BF16)
    widxt_ref[0] = pb[_QIDX_ROWS:_QIDX_ROWS + IDX_HEADS] * (IDX_HEADS ** -0.5 * IDX_HEAD_DIM ** -0.5)


def _dsa_in_call(x, mod, layer, norm_g, j, wa, wbt, wukt, g_kv, g_kidx, b_kidx):
    b, t, d = x.shape
    tm = TOKEN_TILE
    ck = ATTN_KEY_CHUNK
    r = ck // tm
    out_shape = [jax.ShapeDtypeStruct((b, N_HEADS, KV_RANK, t), BF16),
                 jax.ShapeDtypeStruct((b, t, KV_RANK), BF16),
                 jax.ShapeDtypeStruct((b, t // ck, KV_RANK, ck), BF16),
                 jax.ShapeDtypeStruct((b, t, IDX_HEAD_DIM), BF16),
                 jax.ShapeDtypeStruct((b, _QIDX_ROWS, t), BF16),
                 jax.ShapeDtypeStruct((b, IDX_HEADS, t), F32)]
    out_specs = [pl.BlockSpec((1, N_HEADS, KV_RANK, tm), lambda i, k: (i, 0, 0, k)),
                 pl.BlockSpec((1, tm, KV_RANK), lambda i, k: (i, k, 0)),
                 pl.BlockSpec((1, 1, KV_RANK, tm), lambda i, k: (i, k // r, 0, k % r)),
                 pl.BlockSpec((1, tm, IDX_HEAD_DIM), lambda i, k: (i, k, 0)),
                 pl.BlockSpec((1, _QIDX_ROWS, tm), lambda i, k: (i, 0, k)),
                 pl.BlockSpec((1, IDX_HEADS, tm), lambda i, k: (i, 0, k))]
    return pl.pallas_call(
        _dsa_in_kernel,
        grid=(b, t // tm),
        in_specs=[pl.BlockSpec((1, tm, d), lambda i, k: (i, k, 0)),
                  _const_spec((1, d), (0, 0)),
                  _mod_spec(layer, 0, d), _mod_spec(layer, 1, d),
                  _const_spec((None, d, _WA_COLS), (j, 0, 0)),
                  _const_spec((None, _WB_ROWS, d), (j, 0, 0)),
                  _const_spec((None, N_HEADS, KV_RANK, QK_HEAD_DIM), (j, 0, 0, 0)),
                  _const_spec((None, 1, KV_RANK), (j, 0, 0)),
                  _const_spec((None, 1, 128), (j, 0, 0)),
                  _const_spec((None, 1, 128), (j, 0, 0))],
        out_specs=out_specs,
        out_shape=out_shape,
        compiler_params=_params(2),
        name="dsa_in",
    )(x, norm_g, mod, mod, wa, wbt, wukt, g_kv, g_kidx, b_kidx)


def _dsa_attn_kernel(x_ref, gt_ref, qlt_ref, qidxt_ref, widxt_ref, ckv_ref, ckvt_ref, kidx_ref,
                     wuvt_ref, wot_ref, o_ref, keys_scr, acc_scr, m_scr, l_scr, ot_scr, *, top_k, seq_bits):
    tq = x_ref.shape[1]
    ck = ckv_ref.shape[2]
    q0 = pl.program_id(1) * tq
    nk = (q0 + tq + ck - 1) // ck
    qpos = q0 + lax.broadcasted_iota(I32, (1, tq), 1)
    kofs = lax.broadcasted_iota(I32, (ck, 1), 0)
    w_t = widxt_ref[0]

    def score_body(c, carry):
        kc = kidx_ref[0, c]
        s = jnp.zeros((ck, tq), F32)
        for hh in range(IDX_HEADS):
            lg = _dot(kc, qidxt_ref[0, hh * IDX_HEAD_DIM:(hh + 1) * IDX_HEAD_DIM, :])
            s = s + jnp.maximum(lg, 0.0) * w_t[hh:hh + 1, :]
        bits = pltpu.bitcast(s, I32)
        key = jnp.where(bits < 0, bits ^ jnp.int32(0x7FFFFFFF), bits)
        causal = (c * ck + kofs) <= qpos
        keys_scr[c] = jnp.where(causal, key, jnp.int32(INT_MIN))
        return carry

    lax.fori_loop(0, nk, score_body, 0)

    def count(pred):
        def body(c, acc):
            m = pred(keys_scr[c], c).astype(I32)
            return acc + jnp.sum(m.reshape(ck // 8, 8, tq), axis=0)
        acc = lax.fori_loop(0, nk, body, jnp.zeros((8, tq), I32))
        return jnp.sum(acc, axis=0, keepdims=True)

    def count_ge(cand):
        return count(lambda k, c: k >= cand)

    t0 = jnp.where(count_ge(jnp.zeros((1, tq), I32)) >= top_k, jnp.int32(0), jnp.int32(INT_MIN))

    def bit_body(i, t):
        cand = t + jnp.left_shift(jnp.int32(1), 30 - i)
        return jnp.where(count_ge(cand) >= top_k, cand, t)

    thr = lax.fori_loop(0, 31, bit_body, t0)
    thr = jnp.maximum(thr, jnp.int32(INT_MIN + 1))

    n_ge = count_ge(thr)
    has_tie = jnp.max(jnp.where(n_ge > top_k, 1, 0)) > 0

    @pl.when(has_tie)
    def _():
        need = top_k - count_ge(thr + 1)

        def count_eq_before(p):
            return count(lambda k, c: (k == thr) & ((c * ck + kofs) < p))

        def idx_body(i, p):
            cand = p + jnp.left_shift(jnp.int32(1), seq_bits - 1 - i)
            return jnp.where(count_eq_before(cand) < need, cand, p)

        last = lax.fori_loop(0, seq_bits, idx_body, jnp.zeros((1, tq), I32))

        def demote(c, carry):
            k = keys_scr[c]
            drop = (k == thr) & ((c * ck + kofs) > last)
            keys_scr[c] = jnp.where(drop, jnp.int32(INT_MIN), k)
            return carry

        lax.fori_loop(0, nk, demote, 0)

    m_scr[...] = jnp.full(m_scr.shape, NEG_BIG, F32)
    l_scr[...] = jnp.zeros(l_scr.shape, F32)
    acc_scr[...] = jnp.zeros(acc_scr.shape, F32)

    def attn_body(c, carry):
        bias = jnp.where(keys_scr[c] >= thr, 0.0, NEG_BIG)
        kc = ckv_ref[0, c]
        kct = ckvt_ref[0, c]
        for hh in range(N_HEADS):
            s = _dot(kc, qlt_ref[0, hh]) + bias
            m_old = m_scr[hh:hh + 1, :]
            m_new = jnp.maximum(m_old, jnp.max(s, axis=0, keepdims=True))
            p = jnp.exp(s - m_new)
            alpha = jnp.exp(m_old - m_new)
            l_scr[hh:hh + 1, :] = alpha * l_scr[hh:hh + 1, :] + jnp.sum(p, axis=0, keepdims=True)
            acc_scr[hh] = alpha * acc_scr[hh] + _dot(kct, p.astype(BF16))
            m_scr[hh:hh + 1, :] = m_new
        return carry

    lax.fori_loop(0, nk, attn_body, 0)

    for hh in range(N_HEADS):
        o_lat_t = (acc_scr[hh] * (1.0 / l_scr[hh:hh + 1, :])).astype(BF16)
        ot_scr[hh * V_HEAD_DIM:(hh + 1) * V_HEAD_DIM, :] = _dot(wuvt_ref[hh], o_lat_t).astype(BF16)
    y_t = _dot(wot_ref[...], ot_scr[...])
    o_ref[0] = x_ref[0] + gt_ref[0, 0] * y_t.T


def _dsa_attn_call(x, mod, layer, j, qlt, qidxt, widxt, ckv, ckvt, kidx, wuvt, wot):
    b, t, d = x.shape
    tq = ATTN_Q_TILE
    ck = ATTN_KEY_CHUNK
    nc = t // ck
    top_k = min(TOPK_MAX, t // 4)
    seq_bits = (t - 1).bit_length()
    kernel = functools.partial(_dsa_attn_kernel, top_k=top_k, seq_bits=seq_bits)
    return pl.pallas_call(
        kernel,
        grid=(b, t // tq),
        in_specs=[pl.BlockSpec((1, tq, d), lambda i, k: (i, k, 0)),
                  _mod_spec(layer, 2, d),
                  pl.BlockSpec((1, N_HEADS, KV_RANK, tq), lambda i, k: (i, 0, 0, k)),
                  pl.BlockSpec((1, _QIDX_ROWS, tq), lambda i, k: (i, 0, k)),
                  pl.BlockSpec((1, IDX_HEADS, tq), lambda i, k: (i, 0, k)),
                  pl.BlockSpec((1, nc, ck, KV_RANK), lambda i, k: (i, 0, 0, 0)),
                  pl.BlockSpec((1, nc, KV_RANK, ck), lambda i, k: (i, 0, 0, 0)),
                  pl.BlockSpec((1, nc, ck, IDX_HEAD_DIM), lambda i, k: (i, 0, 0, 0)),
                  _const_spec((None, N_HEADS, V_HEAD_DIM, KV_RANK), (j, 0, 0, 0)),
                  _const_spec((None, d, N_HEADS * V_HEAD_DIM), (j, 0, 0))],
        out_specs=pl.BlockSpec((1, tq, d), lambda i, k: (i, k, 0)),
        out_shape=jax.ShapeDtypeStruct((b, t, d), F32),
        scratch_shapes=[pltpu.VMEM((nc, ck, tq), I32),
                        pltpu.VMEM((N_HEADS, KV_RANK, tq), F32),
                        pltpu.VMEM((N_HEADS, tq), F32),
                        pltpu.VMEM((N_HEADS, tq), F32),
                        pltpu.VMEM((N_HEADS * V_HEAD_DIM, tq), BF16)],
        compiler_params=_params(2),
        name="dsa_attn",
    )(x, mod, qlt, qidxt, widxt,
      ckv.reshape(b, nc, ck, KV_RANK), ckvt, kidx.reshape(b, nc, ck, IDX_HEAD_DIM), wuvt, wot)


def _prep_dsa_weights(a_w_in, a_w_uk, a_w_uv, a_w_o, a_g_kidx, a_b_kidx):
    n_a, d, _ = a_w_in.shape
    o1 = N_HEADS * QK_HEAD_DIM
    o2 = o1 + KV_RANK
    o3 = o2 + _QIDX_ROWS
    o4 = o3 + IDX_HEAD_DIM
    pad_cols = jnp.zeros((n_a, d, 128 - IDX_HEAD_DIM), a_w_in.dtype)
    wa = jnp.concatenate([a_w_in[..., :o2], a_w_in[..., o3:o4], pad_cols], axis=-1).astype(BF16)
    pad_rows = jnp.zeros((n_a, _WB_ROWS - _QIDX_ROWS - IDX_HEADS, d), a_w_in.dtype)
    wbt = jnp.concatenate([jnp.swapaxes(a_w_in[..., o2:o3], 1, 2),
                           jnp.swapaxes(a_w_in[..., o4:], 1, 2), pad_rows], axis=1).astype(BF16)
    wukt = jnp.swapaxes(a_w_uk, 2, 3).astype(BF16)
    wuvt = jnp.swapaxes(a_w_uv, 2, 3).astype(BF16)
    wot = jnp.swapaxes(a_w_o, 1, 2).astype(BF16)
    pad = jnp.zeros((n_a, 128 - IDX_HEAD_DIM), F32)
    gk = jnp.concatenate([a_g_kidx, pad], axis=-1)
    bk = jnp.concatenate([a_b_kidx, pad], axis=-1)
    return wa, wbt, wukt, wuvt, wot, gk[:, None, :], bk[:, None, :]


def kernel(x, c, mod_w, mod_b, norm_mix_g, norm_ffn_g, a_w_in, a_g_kv, a_g_kidx, a_b_kidx, a_w_uk, a_w_uv,
           a_w_o, b_w_in, b_ln_g, b_ln_b, b_w_s, b_b_s, b_w_out, ffn_w_gate, ffn_w_up, ffn_w_down, final_g):
    depth = mod_w.shape[0]
    b, t, d = x.shape
    assert t % ATTN_KEY_CHUNK == 0 and ATTN_KEY_CHUNK % TOKEN_TILE == 0 and t % ATTN_Q_TILE == 0

    mod = _mod_call(c, mod_w, mod_b)
    wa, wbt, wukt, wuvt, wot, gk, bk = _prep_dsa_weights(a_w_in, a_w_uk, a_w_uv, a_w_o, a_g_kidx, a_b_kidx)
    b_w_in16 = b_w_in.astype(BF16)
    b_w_out16 = b_w_out.astype(BF16)
    b_bs = b_b_s[..., None]
    wg16 = ffn_w_gate.astype(BF16)
    wu16 = ffn_w_up.astype(BF16)
    wd16 = ffn_w_down.astype(BF16)
    final_g2 = final_g.reshape(1, d)

    for layer in range(depth):
        j = layer // 2
        g_mix = norm_mix_g[layer].reshape(1, d)
        if layer % 2 == 0:
            qlt, ckv, ckvt, kidx, qidxt, widxt = _dsa_in_call(
                x, mod, layer, g_mix, j, wa, wbt, wukt, a_g_kv[:, None, :], gk, bk)
            x = _dsa_attn_call(x, mod, layer, j, qlt, qidxt, widxt, ckv, ckvt, kidx, wuvt, wot)
        else:
            x = _gmlp_call(x, mod, layer, g_mix, j, b_w_in16, b_ln_g[:, None, :], b_ln_b[:, None, :],
                           b_w_s, b_bs, b_w_out16)
        x = _ffn_call(x, mod, layer, norm_ffn_g[layer].reshape(1, d), wg16, wu16, wd16,
                      final_g2 if layer == depth - 1 else None)
    return x
```

```python
import functools

import jax
import jax.numpy as jnp
from jax import lax
from jax.experimental import pallas as pl
from jax.experimental.pallas import tpu as pltpu

F32 = jnp.float32
BF16 = jnp.bfloat16
I32 = jnp.int32
I16 = jnp.int16

EPS = 1e-6
N_MOD = 6
N_HEADS = 8
QK_HEAD_DIM = 128
V_HEAD_DIM = 128
KV_RANK = 256
IDX_HEADS = 8
IDX_HEAD_DIM = 64
TOPK_MAX = 256
GMLP_GROUPS = 8
CHUNK = 128

TOKEN_TILE = 256
ATTN_Q_TILE = 256
ATTN_KEY_CHUNK = 512
VMEM_LIMIT_BYTES = 56 * 1024 * 1024

INT_MIN = -(2 ** 31)
I16_MIN = -(2 ** 15)
LOG2_E = 1.4426950408889634
ONES_ROWS = 16
NEG_BIG = -1e30

_NT_DIMS = (((1,), (1,)), ((), ()))


def _dot(a, b):
    return jnp.dot(a, b, preferred_element_type=F32)


def _dot_nt(a, b):
    return lax.dot_general(a, b, _NT_DIMS, preferred_element_type=F32)


def _norm_mod(x, g, shift, scale):
    ms = jnp.mean(x * x, axis=-1, keepdims=True)
    y = x * lax.rsqrt(ms + EPS) * g
    return y * (1.0 + scale) + shift


def _const_spec(block_shape, index):
    return pl.BlockSpec(block_shape, lambda *_: index, pipeline_mode=pl.Buffered(1))


def _params(n_axes):
    return pltpu.CompilerParams(dimension_semantics=("arbitrary",) * n_axes,
                                vmem_limit_bytes=VMEM_LIMIT_BYTES)


def _mod_kernel(c_ref, w_ref, b_ref, o_ref):
    c = c_ref[...]
    c_act = c * jax.nn.sigmoid(c)
    o_ref[0] = jnp.dot(c_act, w_ref[0], preferred_element_type=F32,
                       precision=lax.Precision.HIGHEST) + b_ref[0]


def _mod_call(c, mod_w, mod_b):
    depth, d, n = mod_w.shape
    b = c.shape[0]
    tn = 1024
    out = pl.pallas_call(
        _mod_kernel,
        grid=(depth, n // tn),
        in_specs=[pl.BlockSpec((b, d), lambda l, j: (0, 0)),
                  pl.BlockSpec((1, d, tn), lambda l, j: (l, 0, j)),
                  pl.BlockSpec((1, 1, tn), lambda l, j: (l, 0, j))],
        out_specs=pl.BlockSpec((1, b, tn), lambda l, j: (l, 0, j)),
        out_shape=jax.ShapeDtypeStruct((depth, b, n), F32),
        compiler_params=_params(2),
        name="mod",
    )(c, mod_w, mod_b.reshape(depth, 1, n))
    return out.reshape(depth, b, 1, n)


def _mod_spec(layer, which, d):
    return pl.BlockSpec((1, 1, 1, d), lambda b, t: (layer, b, 0, which))


def _ffn_kernel(x_ref, g_ref, sh_ref, sc_ref, gt_ref, wg_ref, wu_ref, wd_ref, *rest, final):
    o_ref = rest[-1]
    x = x_ref[0]
    h = _norm_mod(x, g_ref[...], sh_ref[0, 0], sc_ref[0, 0]).astype(BF16)
    gate = _dot(h, wg_ref[...])
    up = _dot(h, wu_ref[...])
    act = (gate * jax.nn.sigmoid(gate) * up).astype(BF16)
    y = _dot(act, wd_ref[...])
    xn = x + gt_ref[0, 0] * y
    if final:
        fg_ref = rest[0]
        ms = jnp.mean(xn * xn, axis=-1, keepdims=True)
        xn = xn * lax.rsqrt(ms + EPS) * fg_ref[...]
    o_ref[0] = xn


def _ffn_call(x, mod, layer, norm_g, wg, wu, wd, final_g):
    b, t, d = x.shape
    dff = wg.shape[-1]
    tm = TOKEN_TILE
    final = final_g is not None
    in_specs = [pl.BlockSpec((1, tm, d), lambda i, j: (i, j, 0)),
                _const_spec((1, d), (0, 0)),
                _mod_spec(layer, 3, d), _mod_spec(layer, 4, d), _mod_spec(layer, 5, d),
                _const_spec((None, d, dff), (layer, 0, 0)),
                _const_spec((None, d, dff), (layer, 0, 0)),
                _const_spec((None, dff, d), (layer, 0, 0))]
    args = [x, norm_g, mod, mod, mod, wg, wu, wd]
    if final:
        in_specs.append(_const_spec((1, d), (0, 0)))
        args.append(final_g)
    return pl.pallas_call(
        functools.partial(_ffn_kernel, final=final),
        grid=(b, t // tm),
        in_specs=in_specs,
        out_specs=pl.BlockSpec((1, tm, d), lambda i, j: (i, j, 0)),
        out_shape=jax.ShapeDtypeStruct((b, t, d), F32),
        compiler_params=_params(2),
        name="ffn",
    )(*args)


def _gmlp_kernel(x_ref, g_ref, sh_ref, sc_ref, gt_ref, win_ref, lng_ref, lnb_ref, ws_ref, bs_ref,
                 wout_ref, o_ref, y_scr):
    tm = x_ref.shape[1]
    half = lng_ref.shape[-1]
    gdim = half // GMLP_GROUPS
    x = x_ref[0]
    h = _norm_mod(x, g_ref[...], sh_ref[0, 0], sc_ref[0, 0]).astype(BF16)
    z = _dot(h, win_ref[...])
    z = 0.5 * z * (1.0 + lax.erf(z * (2.0 ** -0.5)))
    v = z[:, half:]
    mu = jnp.mean(v, axis=-1, keepdims=True)
    vc = v - mu
    var = jnp.mean(vc * vc, axis=-1, keepdims=True)
    vn = (vc * lax.rsqrt(var + EPS) * lng_ref[...] + lnb_ref[...]).astype(BF16)
    row = lax.broadcasted_iota(I32, (CHUNK, CHUNK), 0)
    col = lax.broadcasted_iota(I32, (CHUNK, CHUNK), 1)
    tril = col <= row
    for g in range(GMLP_GROUPS):
        w_causal = jnp.where(tril, ws_ref[g], 0.0).astype(BF16)
        bias = bs_ref[g]
        lo = g * gdim
        for c in range(tm // CHUNK):
            r0 = c * CHUNK
            mixed = _dot(w_causal, vn[r0:r0 + CHUNK, lo:lo + gdim]) + bias
            y_scr[r0:r0 + CHUNK, lo:lo + gdim] = (z[r0:r0 + CHUNK, lo:lo + gdim] * mixed).astype(BF16)
    y = _dot(y_scr[...], wout_ref[...])
    o_ref[0] = x + gt_ref[0, 0] * y


def _gmlp_call(x, mod, layer, norm_g, j, w_in, ln_g, ln_b, w_s, b_s, w_out):
    b, t, d = x.shape
    half = ln_g.shape[-1]
    tm = TOKEN_TILE
    return pl.pallas_call(
        _gmlp_kernel,
        grid=(b, t // tm),
        in_specs=[pl.BlockSpec((1, tm, d), lambda i, k: (i, k, 0)),
                  _const_spec((1, d), (0, 0)),
                  _mod_spec(layer, 0, d), _mod_spec(layer, 1, d), _mod_spec(layer, 2, d),
                  _const_spec((None, d, 2 * half), (j, 0, 0)),
                  _const_spec((None, 1, half), (j, 0, 0)),
                  _const_spec((None, 1, half), (j, 0, 0)),
                  _const_spec((None, GMLP_GROUPS, CHUNK, CHUNK), (j, 0, 0, 0)),
                  _const_spec((None, GMLP_GROUPS, CHUNK, 1), (j, 0, 0, 0)),
                  _const_spec((None, half, d), (j, 0, 0))],
        out_specs=pl.BlockSpec((1, tm, d), lambda i, k: (i, k, 0)),
        out_shape=jax.ShapeDtypeStruct((b, t, d), F32),
        scratch_shapes=[pltpu.VMEM((tm, half), BF16)],
        compiler_params=_params(2),
        name="gmlp",
    )(x, norm_g, mod, mod, mod, w_in, ln_g, ln_b, w_s, b_s, w_out)


_O_CKV = N_HEADS * QK_HEAD_DIM
_O_KIDX = _O_CKV + KV_RANK
_WA_COLS = _O_KIDX + 128
_QIDX_ROWS = IDX_HEADS * IDX_HEAD_DIM
_WB_ROWS = _QIDX_ROWS + 16


def _dsa_in_kernel(x_ref, g_ref, sh_ref, sc_ref, wa_ref, wbt_ref, wukt_ref, gkv_ref, gk_ref, bk_ref,
                   qlt_ref, ckv_ref, ckvt_ref, kidx_ref, qidxt_ref, widxt_ref):
    x = x_ref[0]
    h = _norm_mod(x, g_ref[...], sh_ref[0, 0], sc_ref[0, 0]).astype(BF16)
    pa = _dot(h, wa_ref[...])
    for hh in range(N_HEADS):
        qh = pa[:, hh * QK_HEAD_DIM:(hh + 1) * QK_HEAD_DIM].astype(BF16)
        qlt = _dot_nt(wukt_ref[hh], qh) * (QK_HEAD_DIM ** -0.5 * LOG2_E)
        qlt_ref[0, hh] = qlt.astype(BF16)
    ck = pa[:, _O_CKV:_O_KIDX]
    ckn = ck * lax.rsqrt(jnp.mean(ck * ck, axis=-1, keepdims=True) + EPS) * gkv_ref[...]
    ckv_ref[0] = ckn.astype(BF16)
    ckvt_ref[0, 0, :KV_RANK, :] = ckn.T.astype(BF16)
    ckvt_ref[0, 0, KV_RANK:, :] = jnp.ones((ONES_ROWS, ckn.shape[0]), BF16)
    kr = pa[:, _O_KIDX:_O_KIDX + 128]
    lane = lax.broadcasted_iota(I32, kr.shape, 1)
    real = lane < IDX_HEAD_DIM
    mu = jnp.sum(jnp.where(real, kr, 0.0), axis=-1, keepdims=True) * (1.0 / IDX_HEAD_DIM)
    kc = jnp.where(real, kr - mu, 0.0)
    var = jnp.sum(kc * kc, axis=-1, keepdims=True) * (1.0 / IDX_HEAD_DIM)
    kn = kc * lax.rsqrt(var + EPS) * gk_ref[...] + bk_ref[...]
    kidx_ref[0] = kn[:, :IDX_HEAD_DIM].astype(BF16)
    pb = _dot_nt(wbt_ref[...], h)
    qidxt_ref[0] = pb[:_QIDX_ROWS].astype(BF16)
    widxt_ref[0] = pb[_QIDX_ROWS:_QIDX_ROWS + IDX_HEADS] * (IDX_HEADS ** -0.5 * IDX_HEAD_DIM ** -0.5)


def _dsa_in_call(x, mod, layer, norm_g, j, wa, wbt, wukt, g_kv, g_kidx, b_kidx):
    b, t, d = x.shape
    tm = TOKEN_TILE
    ck = ATTN_KEY_CHUNK
    r = ck // tm
    out_shape = [jax.ShapeDtypeStruct((b, N_HEADS, KV_RANK, t), BF16),
                 jax.ShapeDtypeStruct((b, t, KV_RANK), BF16),
                 jax.ShapeDtypeStruct((b, t // ck, KV_RANK + ONES_ROWS, ck), BF16),
                 jax.ShapeDtypeStruct((b, t, IDX_HEAD_DIM), BF16),
                 jax.ShapeDtypeStruct((b, _QIDX_ROWS, t), BF16),
                 jax.ShapeDtypeStruct((b, IDX_HEADS, t), F32)]
    out_specs = [pl.BlockSpec((1, N_HEADS, KV_RANK, tm), lambda i, k: (i, 0, 0, k)),
                 pl.BlockSpec((1, tm, KV_RANK), lambda i, k: (i, k, 0)),
                 pl.BlockSpec((1, 1, KV_RANK + ONES_ROWS, tm), lambda i, k: (i, k // r, 0, k % r)),
                 pl.BlockSpec((1, tm, IDX_HEAD_DIM), lambda i, k: (i, k, 0)),
                 pl.BlockSpec((1, _QIDX_ROWS, tm), lambda i, k: (i, 0, k)),
                 pl.BlockSpec((1, IDX_HEADS, tm), lambda i, k: (i, 0, k))]
    return pl.pallas_call(
        _dsa_in_kernel,
        grid=(b, t // tm),
        in_specs=[pl.BlockSpec((1, tm, d), lambda i, k: (i, k, 0)),
                  _const_spec((1, d), (0, 0)),
                  _mod_spec(layer, 0, d), _mod_spec(layer, 1, d),
                  _const_spec((None, d, _WA_COLS), (j, 0, 0)),
                  _const_spec((None, _WB_ROWS, d), (j, 0, 0)),
                  _const_spec((None, N_HEADS, KV_RANK, QK_HEAD_DIM), (j, 0, 0, 0)),
                  _const_spec((None, 1, KV_RANK), (j, 0, 0)),
                  _const_spec((None, 1, 128), (j, 0, 0)),
                  _const_spec((None, 1, 128), (j, 0, 0))],
        out_specs=out_specs,
        out_shape=out_shape,
        compiler_params=_params(2),
        name="dsa_in",
    )(x, norm_g, mod, mod, wa, wbt, wukt, g_kv, g_kidx, b_kidx)


def _dsa_attn_kernel(x_ref, gt_ref, qlt_ref, qidxt_ref, widxt_ref, ckv_ref, ckvt_ref, kidx_ref,
                     wuvt_ref, wot_ref, o_ref, keys_scr, half_scr, acc_scr, m_scr, ot_scr, *, top_k, seq_bits):
    tq = x_ref.shape[1]
    ck = ckv_ref.shape[2]
    q0 = pl.program_id(1) * tq
    nk = (q0 + tq + ck - 1) // ck
    qpos = q0 + lax.broadcasted_iota(I32, (1, tq), 1)
    kofs = lax.broadcasted_iota(I32, (ck, 1), 0)
    w_t = widxt_ref[0]

    def score_body(c, carry):
        kc = kidx_ref[0, c]
        s = jnp.zeros((ck, tq), F32)
        for hh in range(IDX_HEADS):
            lg = _dot(kc, qidxt_ref[0, hh * IDX_HEAD_DIM:(hh + 1) * IDX_HEAD_DIM, :])
            s = s + jnp.maximum(lg, 0.0) * w_t[hh:hh + 1, :]
        bits = pltpu.bitcast(s, I32)
        key = jnp.where(bits < 0, bits ^ jnp.int32(0x7FFFFFFF), bits)
        causal = (c * ck + kofs) <= qpos
        key = jnp.where(causal, key, jnp.int32(INT_MIN))
        keys_scr[c] = key
        half_scr[c] = jnp.right_shift(key, 16).astype(I16)
        return carry

    lax.fori_loop(0, nk, score_body, 0)

    def count(src_scr, one, pred):
        def body(c, acc):
            m = pred(src_scr[c], c)
            for j in range(ck // 64):
                acc = jnp.where(m[j * 64:(j + 1) * 64], acc + one, acc)
            return acc
        acc = lax.fori_loop(0, nk, body, jnp.zeros((64, tq), one.dtype))
        return jnp.sum(acc.astype(I32), axis=0, keepdims=True)

    def count_ge(cand):
        return count(keys_scr, jnp.int32(1), lambda k, c: k >= cand)

    def count16(pred):
        return count(half_scr, jnp.int16(1), pred)

    def search16(target):
        def bit_body(i, t):
            cand = t + jnp.left_shift(jnp.int32(1), 15 - i)
            c16 = cand.astype(I16)
            return jnp.where(count16(lambda v, c: v >= c16) >= target, cand, t)
        return lax.fori_loop(0, 16, bit_body, jnp.full((1, tq), I16_MIN, I32))

    t_hi = search16(top_k)
    t_hi16 = t_hi.astype(I16)
    n_above = count16(lambda v, c: v > t_hi16)

    def low_half_body(c, carry):
        k = keys_scr[c]
        match = jnp.right_shift(k, 16) == t_hi
        low = jnp.bitwise_and(k, 0xFFFF) + I16_MIN
        half_scr[c] = jnp.where(match, low, I16_MIN).astype(I16)
        return carry

    lax.fori_loop(0, nk, low_half_body, 0)
    t_lo = search16(top_k - n_above)
    thr = jnp.left_shift(t_hi, 16) + (t_lo - I16_MIN)
    thr = jnp.maximum(thr, jnp.int32(INT_MIN + 1))

    n_ge = count_ge(thr)
    has_tie = jnp.max(jnp.where(n_ge > top_k, 1, 0)) > 0

    @pl.when(has_tie)
    def _():
        need = top_k - count_ge(thr + 1)

        def count_eq_before(p):
            return count(keys_scr, jnp.int32(1), lambda k, c: (k == thr) & ((c * ck + kofs) < p))

        def idx_body(i, p):
            cand = p + jnp.left_shift(jnp.int32(1), seq_bits - 1 - i)
            return jnp.where(count_eq_before(cand) < need, cand, p)

        last = lax.fori_loop(0, seq_bits, idx_body, jnp.zeros((1, tq), I32))

        def demote(c, carry):
            k = keys_scr[c]
            drop = (k == thr) & ((c * ck + kofs) > last)
            keys_scr[c] = jnp.where(drop, jnp.int32(INT_MIN), k)
            return carry

        lax.fori_loop(0, nk, demote, 0)

    m_scr[...] = jnp.full(m_scr.shape, NEG_BIG, F32)
    acc_scr[...] = jnp.zeros(acc_scr.shape, F32)

    def attn_body(c, carry):
        bias = jnp.where(keys_scr[c] >= thr, 0.0, NEG_BIG)
        kc = ckv_ref[0, c]
        kct = ckvt_ref[0, c]
        s_next = _dot(kc, qlt_ref[0, 0]) + bias
        for hh in range(N_HEADS):
            s = s_next
            if hh + 1 < N_HEADS:
                s_next = _dot(kc, qlt_ref[0, hh + 1]) + bias
            m_old = m_scr[hh:hh + 1, :]
            m_new = jnp.maximum(m_old, jnp.max(s, axis=0, keepdims=True))
            p = jnp.exp2(s - m_new)
            alpha = jnp.exp2(m_old - m_new)
            acc_scr[hh] = alpha * acc_scr[hh] + _dot(kct, p.astype(BF16))
            m_scr[hh:hh + 1, :] = m_new
        return carry

    lax.fori_loop(0, nk, attn_body, 0)

    for hh in range(N_HEADS):
        inv_l = 1.0 / acc_scr[hh, KV_RANK:KV_RANK + 1, :]
        o_lat_t = (acc_scr[hh, :KV_RANK, :] * inv_l).astype(BF16)
        ot_scr[hh * V_HEAD_DIM:(hh + 1) * V_HEAD_DIM, :] = _dot(wuvt_ref[hh], o_lat_t).astype(BF16)
    y_t = _dot(wot_ref[...], ot_scr[...])
    o_ref[0] = x_ref[0] + gt_ref[0, 0] * y_t.T


def _dsa_attn_call(x, mod, layer, j, qlt, qidxt, widxt, ckv, ckvt, kidx, wuvt, wot):
    b, t, d = x.shape
    tq = ATTN_Q_TILE
    ck = ATTN_KEY_CHUNK
    nc = t // ck
    top_k = min(TOPK_MAX, t // 4)
    seq_bits = (t - 1).bit_length()
    kernel = functools.partial(_dsa_attn_kernel, top_k=top_k, seq_bits=seq_bits)
    return pl.pallas_call(
        kernel,
        grid=(b, t // tq),
        in_specs=[pl.BlockSpec((1, tq, d), lambda i, k: (i, k, 0)),
                  _mod_spec(layer, 2, d),
                  pl.BlockSpec((1, N_HEADS, KV_RANK, tq), lambda i, k: (i, 0, 0, k)),
                  pl.BlockSpec((1, _QIDX_ROWS, tq), lambda i, k: (i, 0, k)),
                  pl.BlockSpec((1, IDX_HEADS, tq), lambda i, k: (i, 0, k)),
                  pl.BlockSpec((1, nc, ck, KV_RANK), lambda i, k: (i, 0, 0, 0)),
                  pl.BlockSpec((1, nc, KV_RANK + ONES_ROWS, ck), lambda i, k: (i, 0, 0, 0)),
                  pl.BlockSpec((1, nc, ck, IDX_HEAD_DIM), lambda i, k: (i, 0, 0, 0)),
                  _const_spec((None, N_HEADS, V_HEAD_DIM, KV_RANK), (j, 0, 0, 0)),
                  _const_spec((None, d, N_HEADS * V_HEAD_DIM), (j, 0, 0))],
        out_specs=pl.BlockSpec((1, tq, d), lambda i, k: (i, k, 0)),
        out_shape=jax.ShapeDtypeStruct((b, t, d), F32),
        scratch_shapes=[pltpu.VMEM((nc, ck, tq), I32),
                        pltpu.VMEM((nc, ck, tq), I16),
                        pltpu.VMEM((N_HEADS, KV_RANK + ONES_ROWS, tq), F32),
                        pltpu.VMEM((N_HEADS, tq), F32),
                        pltpu.VMEM((N_HEADS * V_HEAD_DIM, tq), BF16)],
        compiler_params=_params(2),
        name="dsa_attn",
    )(x, mod, qlt, qidxt, widxt,
      ckv.reshape(b, nc, ck, KV_RANK), ckvt, kidx.reshape(b, nc, ck, IDX_HEAD_DIM), wuvt, wot)


def _prep_dsa_weights(a_w_in, a_w_uk, a_w_uv, a_w_o, a_g_kidx, a_b_kidx):
    n_a, d, _ = a_w_in.shape
    o1 = N_HEADS * QK_HEAD_DIM
    o2 = o1 + KV_RANK
    o3 = o2 + _QIDX_ROWS
    o4 = o3 + IDX_HEAD_DIM
    pad_cols = jnp.zeros((n_a, d, 128 - IDX_HEAD_DIM), a_w_in.dtype)
    wa = jnp.concatenate([a_w_in[..., :o2], a_w_in[..., o3:o4], pad_cols], axis=-1).astype(BF16)
    pad_rows = jnp.zeros((n_a, _WB_ROWS - _QIDX_ROWS - IDX_HEADS, d), a_w_in.dtype)
    wbt = jnp.concatenate([jnp.swapaxes(a_w_in[..., o2:o3], 1, 2),
                           jnp.swapaxes(a_w_in[..., o4:], 1, 2), pad_rows], axis=1).astype(BF16)
    wukt = jnp.swapaxes(a_w_uk, 2, 3).astype(BF16)
    wuvt = jnp.swapaxes(a_w_uv, 2, 3).astype(BF16)
    wot = jnp.swapaxes(a_w_o, 1, 2).astype(BF16)
    pad = jnp.zeros((n_a, 128 - IDX_HEAD_DIM), F32)
    gk = jnp.concatenate([a_g_kidx, pad], axis=-1)
    bk = jnp.concatenate([a_b_kidx, pad], axis=-1)
    return wa, wbt, wukt, wuvt, wot, gk[:, None, :], bk[:, None, :]


def kernel(x, c, mod_w, mod_b, norm_mix_g, norm_ffn_g, a_w_in, a_g_kv, a_g_kidx, a_b_kidx, a_w_uk, a_w_uv,
           a_w_o, b_w_in, b_ln_g, b_ln_b, b_w_s, b_b_s, b_w_out, ffn_w_gate, ffn_w_up, ffn_w_down, final_g):
    depth = mod_w.shape[0]
    b, t, d = x.shape
    assert t % ATTN_KEY_CHUNK == 0 and ATTN_KEY_CHUNK % TOKEN_TILE == 0 and t % ATTN_Q_TILE == 0

    mod = _mod_call(c, mod_w, mod_b)
    wa, wbt, wukt, wuvt, wot, gk, bk = _prep_dsa_weights(a_w_in, a_w_uk, a_w_uv, a_w_o, a_g_kidx, a_b_kidx)
    b_w_in16 = b_w_in.astype(BF16)
    b_w_out16 = b_w_out.astype(BF16)
    b_bs = b_b_s[..., None]
    wg16 = ffn_w_gate.astype(BF16)
    wu16 = ffn_w_up.astype(BF16)
    wd16 = ffn_w_down.astype(BF16)
    final_g2 = final_g.reshape(1, d)

    for layer in range(depth):
        j = layer // 2
        g_mix = norm_mix_g[layer].reshape(1, d)
        if layer % 2 == 0:
            qlt, ckv, ckvt, kidx, qidxt, widxt = _dsa_in_call(
                x, mod, layer, g_mix, j, wa, wbt, wukt, a_g_kv[:, None, :], gk, bk)
            x = _dsa_attn_call(x, mod, layer, j, qlt, qidxt, widxt, ckv, ckvt, kidx, wuvt, wot)
        else:
            x = _gmlp_call(x, mod, layer, g_mix, j, b_w_in16, b_ln_g[:, None, :], b_ln_b[:, None, :],
                           b_w_s, b_bs, b_w_out16)
        x = _ffn_call(x, mod, layer, norm_ffn_g[layer].reshape(1, d), wg16, wu16, wd16,
                      final_g2 if layer == depth - 1 else None)
    return x
```

```python
import functools

import jax
import jax.numpy as jnp
from jax import lax
from jax.experimental import pallas as pl
from jax.experimental.pallas import tpu as pltpu

F32 = jnp.float32
BF16 = jnp.bfloat16
I32 = jnp.int32
I16 = jnp.int16

EPS = 1e-6
N_MOD = 6
N_HEADS = 8
QK_HEAD_DIM = 128
V_HEAD_DIM = 128
KV_RANK = 256
IDX_HEADS = 8
IDX_HEAD_DIM = 64
TOPK_MAX = 256
GMLP_GROUPS = 8
CHUNK = 128

TOKEN_TILE = 256
FFN_TOKEN_TILE = 512
ATTN_Q_TILE = 256
ATTN_KEY_CHUNK = 512
VMEM_LIMIT_BYTES = 56 * 1024 * 1024

INT_MIN = -(2 ** 31)
I16_MIN = -(2 ** 15)
LOG2_E = 1.4426950408889634
ONES_ROWS = 16
NEG_BIG = -1e30

_NT_DIMS = (((1,), (1,)), ((), ()))


def _dot(a, b):
    return jnp.dot(a, b, preferred_element_type=F32)


def _dot_nt(a, b):
    return lax.dot_general(a, b, _NT_DIMS, preferred_element_type=F32)


def _norm_mod(x, g, shift, scale):
    ms = jnp.mean(x * x, axis=-1, keepdims=True)
    y = x * lax.rsqrt(ms + EPS) * g
    return y * (1.0 + scale) + shift


def _const_spec(block_shape, index):
    return pl.BlockSpec(block_shape, lambda *_: index, pipeline_mode=pl.Buffered(1))


def _params(n_axes):
    return pltpu.CompilerParams(dimension_semantics=("arbitrary",) * n_axes,
                                vmem_limit_bytes=VMEM_LIMIT_BYTES)


def _mod_kernel(c_ref, w_ref, b_ref, o_ref):
    c = c_ref[...]
    c_act = c * jax.nn.sigmoid(c)
    o_ref[0] = jnp.dot(c_act, w_ref[0], preferred_element_type=F32,
                       precision=lax.Precision.HIGHEST) + b_ref[0]


def _mod_call(c, mod_w, mod_b):
    depth, d, n = mod_w.shape
    b = c.shape[0]
    tn = 1024
    out = pl.pallas_call(
        _mod_kernel,
        grid=(depth, n // tn),
        in_specs=[pl.BlockSpec((b, d), lambda l, j: (0, 0)),
                  pl.BlockSpec((1, d, tn), lambda l, j: (l, 0, j)),
                  pl.BlockSpec((1, 1, tn), lambda l, j: (l, 0, j))],
        out_specs=pl.BlockSpec((1, b, tn), lambda l, j: (l, 0, j)),
        out_shape=jax.ShapeDtypeStruct((depth, b, n), F32),
        compiler_params=_params(2),
        name="mod",
    )(c, mod_w, mod_b.reshape(depth, 1, n))
    return out.reshape(depth, b, 1, n)


def _mod_spec(layer, which, d):
    return pl.BlockSpec((1, 1, 1, d), lambda b, t: (layer, b, 0, which))


def _ffn_kernel(x_ref, g_ref, sh_ref, sc_ref, gt_ref, wg_ref, wu_ref, wd_ref, *rest, final):
    o_ref = rest[-1]
    x = x_ref[0]
    h = _norm_mod(x, g_ref[...], sh_ref[0, 0], sc_ref[0, 0]).astype(BF16)
    gate = _dot(h, wg_ref[...])
    up = _dot(h, wu_ref[...])
    act = (gate * jax.nn.sigmoid(gate) * up).astype(BF16)
    y = _dot(act, wd_ref[...])
    xn = x + gt_ref[0, 0] * y
    if final:
        fg_ref = rest[0]
        ms = jnp.mean(xn * xn, axis=-1, keepdims=True)
        xn = xn * lax.rsqrt(ms + EPS) * fg_ref[...]
    o_ref[0] = xn


def _ffn_call(x, mod, layer, norm_g, wg, wu, wd, final_g):
    b, t, d = x.shape
    dff = wg.shape[-1]
    tm = FFN_TOKEN_TILE
    final = final_g is not None
    in_specs = [pl.BlockSpec((1, tm, d), lambda i, j: (i, j, 0)),
                _const_spec((1, d), (0, 0)),
                _mod_spec(layer, 3, d), _mod_spec(layer, 4, d), _mod_spec(layer, 5, d),
                _const_spec((None, d, dff), (layer, 0, 0)),
                _const_spec((None, d, dff), (layer, 0, 0)),
                _const_spec((None, dff, d), (layer, 0, 0))]
    args = [x, norm_g, mod, mod, mod, wg, wu, wd]
    if final:
        in_specs.append(_const_spec((1, d), (0, 0)))
        args.append(final_g)
    return pl.pallas_call(
        functools.partial(_ffn_kernel, final=final),
        grid=(b, t // tm),
        in_specs=in_specs,
        out_specs=pl.BlockSpec((1, tm, d), lambda i, j: (i, j, 0)),
        out_shape=jax.ShapeDtypeStruct((b, t, d), F32),
        compiler_params=_params(2),
        name="ffn",
    )(*args)


def _gmlp_kernel(x_ref, g_ref, sh_ref, sc_ref, gt_ref, win_ref, lng_ref, lnb_ref, ws_ref, bs_ref,
                 wout_ref, o_ref, y_scr):
    tm = x_ref.shape[1]
    half = lng_ref.shape[-1]
    gdim = half // GMLP_GROUPS
    x = x_ref[0]
    h = _norm_mod(x, g_ref[...], sh_ref[0, 0], sc_ref[0, 0]).astype(BF16)
    z = _dot(h, win_ref[...])
    z = 0.5 * z * (1.0 + lax.erf(z * (2.0 ** -0.5)))
    v = z[:, half:]
    mu = jnp.mean(v, axis=-1, keepdims=True)
    vc = v - mu
    var = jnp.mean(vc * vc, axis=-1, keepdims=True)
    vn = (vc * lax.rsqrt(var + EPS) * lng_ref[...] + lnb_ref[...]).astype(BF16)
    row = lax.broadcasted_iota(I32, (CHUNK, CHUNK), 0)
    col = lax.broadcasted_iota(I32, (CHUNK, CHUNK), 1)
    tril = col <= row
    for g in range(GMLP_GROUPS):
        w_causal = jnp.where(tril, ws_ref[g], 0.0).astype(BF16)
        bias = bs_ref[g]
        lo = g * gdim
        for c in range(tm // CHUNK):
            r0 = c * CHUNK
            mixed = _dot(w_causal, vn[r0:r0 + CHUNK, lo:lo + gdim]) + bias
            y_scr[r0:r0 + CHUNK, lo:lo + gdim] = (z[r0:r0 + CHUNK, lo:lo + gdim] * mixed).astype(BF16)
    y = _dot(y_scr[...], wout_ref[...])
    o_ref[0] = x + gt_ref[0, 0] * y


def _gmlp_call(x, mod, layer, norm_g, j, w_in, ln_g, ln_b, w_s, b_s, w_out):
    b, t, d = x.shape
    half = ln_g.shape[-1]
    tm = FFN_TOKEN_TILE
    return pl.pallas_call(
        _gmlp_kernel,
        grid=(b, t // tm),
        in_specs=[pl.BlockSpec((1, tm, d), lambda i, k: (i, k, 0)),
                  _const_spec((1, d), (0, 0)),
                  _mod_spec(layer, 0, d), _mod_spec(layer, 1, d), _mod_spec(layer, 2, d),
                  _const_spec((None, d, 2 * half), (j, 0, 0)),
                  _const_spec((None, 1, half), (j, 0, 0)),
                  _const_spec((None, 1, half), (j, 0, 0)),
                  _const_spec((None, GMLP_GROUPS, CHUNK, CHUNK), (j, 0, 0, 0)),
                  _const_spec((None, GMLP_GROUPS, CHUNK, 1), (j, 0, 0, 0)),
                  _const_spec((None, half, d), (j, 0, 0))],
        out_specs=pl.BlockSpec((1, tm, d), lambda i, k: (i, k, 0)),
        out_shape=jax.ShapeDtypeStruct((b, t, d), F32),
        scratch_shapes=[pltpu.VMEM((tm, half), BF16)],
        compiler_params=_params(2),
        name="gmlp",
    )(x, norm_g, mod, mod, mod, w_in, ln_g, ln_b, w_s, b_s, w_out)


_O_CKV = N_HEADS * QK_HEAD_DIM
_O_KIDX = _O_CKV + KV_RANK
_WA_COLS = _O_KIDX + 128
_QIDX_ROWS = IDX_HEADS * IDX_HEAD_DIM
_WB_ROWS = _QIDX_ROWS + 16


def _dsa_in_kernel(x_ref, g_ref, sh_ref, sc_ref, wa_ref, wbt_ref, wukt_ref, gkv_ref, gk_ref, bk_ref,
                   qlt_ref, ckv_ref, ckvt_ref, kidx_ref, qidxt_ref, widxt_ref):
    x = x_ref[0]
    h = _norm_mod(x, g_ref[...], sh_ref[0, 0], sc_ref[0, 0]).astype(BF16)
    pa = _dot(h, wa_ref[...])
    for hh in range(N_HEADS):
        qh = pa[:, hh * QK_HEAD_DIM:(hh + 1) * QK_HEAD_DIM].astype(BF16)
        qlt = _dot_nt(wukt_ref[hh], qh) * (QK_HEAD_DIM ** -0.5 * LOG2_E)
        qlt_ref[0, hh] = qlt.astype(BF16)
    ck = pa[:, _O_CKV:_O_KIDX]
    ckn = ck * lax.rsqrt(jnp.mean(ck * ck, axis=-1, keepdims=True) + EPS) * gkv_ref[...]
    ckv_ref[0] = ckn.astype(BF16)
    ckvt_ref[0, 0, :KV_RANK, :] = ckn.T.astype(BF16)
    ckvt_ref[0, 0, KV_RANK:, :] = jnp.ones((ONES_ROWS, ckn.shape[0]), BF16)
    kr = pa[:, _O_KIDX:_O_KIDX + 128]
    lane = lax.broadcasted_iota(I32, kr.shape, 1)
    real = lane < IDX_HEAD_DIM
    mu = jnp.sum(jnp.where(real, kr, 0.0), axis=-1, keepdims=True) * (1.0 / IDX_HEAD_DIM)
    kc = jnp.where(real, kr - mu, 0.0)
    var = jnp.sum(kc * kc, axis=-1, keepdims=True) * (1.0 / IDX_HEAD_DIM)
    kn = kc * lax.rsqrt(var + EPS) * gk_ref[...] + bk_ref[...]
    kidx_ref[0] = kn[:, :IDX_HEAD_DIM].astype(BF16)
    pb = _dot_nt(wbt_ref[...], h)
    qidxt_ref[0] = pb[:_QIDX_ROWS].astype(BF16)
    widxt_ref[0] = pb[_QIDX_ROWS:_QIDX_ROWS + IDX_HEADS] * (IDX_HEADS ** -0.5 * IDX_HEAD_DIM ** -0.5)


def _dsa_in_call(x, mod, layer, norm_g, j, wa, wbt, wukt, g_kv, g_kidx, b_kidx):
    b, t, d = x.shape
    tm = TOKEN_TILE
    ck = ATTN_KEY_CHUNK
    r = ck // tm
    out_shape = [jax.ShapeDtypeStruct((b, N_HEADS, KV_RANK, t), BF16),
                 jax.ShapeDtypeStruct((b, t, KV_RANK), BF16),
                 jax.ShapeDtypeStruct((b, t // ck, KV_RANK + ONES_ROWS, ck), BF16),
                 jax.ShapeDtypeStruct((b, t, IDX_HEAD_DIM), BF16),
                 jax.ShapeDtypeStruct((b, _QIDX_ROWS, t), BF16),
                 jax.ShapeDtypeStruct((b, IDX_HEADS, t), F32)]
    out_specs = [pl.BlockSpec((1, N_HEADS, KV_RANK, tm), lambda i, k: (i, 0, 0, k)),
                 pl.BlockSpec((1, tm, KV_RANK), lambda i, k: (i, k, 0)),
                 pl.BlockSpec((1, 1, KV_RANK + ONES_ROWS, tm), lambda i, k: (i, k // r, 0, k % r)),
                 pl.BlockSpec((1, tm, IDX_HEAD_DIM), lambda i, k: (i, k, 0)),
                 pl.BlockSpec((1, _QIDX_ROWS, tm), lambda i, k: (i, 0, k)),
                 pl.BlockSpec((1, IDX_HEADS, tm), lambda i, k: (i, 0, k))]
    return pl.pallas_call(
        _dsa_in_kernel,
        grid=(b, t // tm),
        in_specs=[pl.BlockSpec((1, tm, d), lambda i, k: (i, k, 0)),
                  _const_spec((1, d), (0, 0)),
                  _mod_spec(layer, 0, d), _mod_spec(layer, 1, d),
                  _const_spec((None, d, _WA_COLS), (j, 0, 0)),
                  _const_spec((None, _WB_ROWS, d), (j, 0, 0)),
                  _const_spec((None, N_HEADS, KV_RANK, QK_HEAD_DIM), (j, 0, 0, 0)),
                  _const_spec((None, 1, KV_RANK), (j, 0, 0)),
                  _const_spec((None, 1, 128), (j, 0, 0)),
                  _const_spec((None, 1, 128), (j, 0, 0))],
        out_specs=out_specs,
        out_shape=out_shape,
        compiler_params=_params(2),
        name="dsa_in",
    )(x, norm_g, mod, mod, wa, wbt, wukt, g_kv, g_kidx, b_kidx)


def _dsa_attn_kernel(x_ref, gt_ref, qlt_ref, qidxt_ref, widxt_ref, ckv_ref, ckvt_ref, kidx_ref,
                     wuvt_ref, wot_ref, o_ref, keys_scr, half_scr, acc_scr, m_scr, ot_scr, *, top_k, seq_bits):
    tq = x_ref.shape[1]
    ck = ckv_ref.shape[2]
    q0 = pl.program_id(1) * tq
    nk = (q0 + tq + ck - 1) // ck
    qpos = q0 + lax.broadcasted_iota(I32, (1, tq), 1)
    kofs = lax.broadcasted_iota(I32, (ck, 1), 0)
    w_t = widxt_ref[0]

    def score_body(c, carry):
        kc = kidx_ref[0, c]
        s = jnp.zeros((ck, tq), F32)
        for hh in range(IDX_HEADS):
            lg = _dot(kc, qidxt_ref[0, hh * IDX_HEAD_DIM:(hh + 1) * IDX_HEAD_DIM, :])
            s = s + jnp.maximum(lg, 0.0) * w_t[hh:hh + 1, :]
        bits = pltpu.bitcast(s, I32)
        key = jnp.where(bits < 0, bits ^ jnp.int32(0x7FFFFFFF), bits)
        causal = (c * ck + kofs) <= qpos
        key = jnp.where(causal, key, jnp.int32(INT_MIN))
        keys_scr[c] = key
        half_scr[c] = jnp.right_shift(key, 16).astype(I16)
        return carry

    lax.fori_loop(0, nk, score_body, 0)

    def count(src_scr, one, pred):
        def body(c, acc):
            m = pred(src_scr[c], c)
            for r in range(ck // 64):
                acc = jnp.where(m[r * 64:(r + 1) * 64], acc + one, acc)
            return acc
        acc = lax.fori_loop(0, nk, body, jnp.zeros((64, tq), one.dtype))
        return jnp.sum(acc.astype(I32), axis=0, keepdims=True)

    def count_ge(cand):
        return count(keys_scr, jnp.int32(1), lambda k, c: k >= cand)

    def count16(pred):
        return count(half_scr, jnp.int16(1), pred)

    def search16(target):
        def bit_body(i, t):
            cand = t + jnp.left_shift(jnp.int32(1), 15 - i)
            c16 = cand.astype(I16)
            return jnp.where(count16(lambda v, c: v >= c16) >= target, cand, t)
        return lax.fori_loop(0, 16, bit_body, jnp.full((1, tq), I16_MIN, I32))

    t_hi = search16(top_k)
    t_hi16 = t_hi.astype(I16)
    n_above = count16(lambda v, c: v > t_hi16)

    def low_half_body(c, carry):
        k = keys_scr[c]
        match = jnp.right_shift(k, 16) == t_hi
        low = jnp.bitwise_and(k, 0xFFFF) + I16_MIN
        half_scr[c] = jnp.where(match, low, I16_MIN).astype(I16)
        return carry

    lax.fori_loop(0, nk, low_half_body, 0)
    t_lo = search16(top_k - n_above)
    thr = jnp.left_shift(t_hi, 16) + (t_lo - I16_MIN)
    thr = jnp.maximum(thr, jnp.int32(INT_MIN + 1))

    n_ge = count_ge(thr)
    has_tie = jnp.max(jnp.where(n_ge > top_k, 1, 0)) > 0

    @pl.when(has_tie)
    def _():
        need = top_k - count_ge(thr + 1)

        def count_eq_before(p):
            return count(keys_scr, jnp.int32(1), lambda k, c: (k == thr) & ((c * ck + kofs) < p))

        def idx_body(i, p):
            cand = p + jnp.left_shift(jnp.int32(1), seq_bits - 1 - i)
            return jnp.where(count_eq_before(cand) < need, cand, p)

        last = lax.fori_loop(0, seq_bits, idx_body, jnp.zeros((1, tq), I32))

        def demote(c, carry):
            k = keys_scr[c]
            drop = (k == thr) & ((c * ck + kofs) > last)
            keys_scr[c] = jnp.where(drop, jnp.int32(INT_MIN), k)
            return carry

        lax.fori_loop(0, nk, demote, 0)

    m_scr[...] = jnp.full(m_scr.shape, NEG_BIG, F32)
    acc_scr[...] = jnp.zeros(acc_scr.shape, F32)

    def attn_body(c, carry):
        bias = jnp.where(keys_scr[c] >= thr, 0.0, NEG_BIG)
        kc = ckv_ref[0, c]
        kct = ckvt_ref[0, c]
        scores = [_dot(kc, qlt_ref[0, hh]) + bias for hh in range(N_HEADS)]
        for hh in range(N_HEADS):
            s = scores[hh]
            m_old = m_scr[hh:hh + 1, :]
            m_new = jnp.maximum(m_old, jnp.max(s, axis=0, keepdims=True))
            p = jnp.exp2(s - m_new)
            alpha = jnp.exp2(m_old - m_new)
            acc_scr[hh] = alpha * acc_scr[hh] + _dot(kct, p.astype(BF16))
            m_scr[hh:hh + 1, :] = m_new
        return carry

    lax.fori_loop(0, nk, attn_body, 0)

    for hh in range(N_HEADS):
        inv_l = 1.0 / acc_scr[hh, KV_RANK:KV_RANK + 1, :]
        o_lat_t = (acc_scr[hh, :KV_RANK, :] * inv_l).astype(BF16)
        ot_scr[hh * V_HEAD_DIM:(hh + 1) * V_HEAD_DIM, :] = _dot(wuvt_ref[hh], o_lat_t).astype(BF16)
    y_t = _dot(wot_ref[...], ot_scr[...])
    o_ref[0] = x_ref[0] + gt_ref[0, 0] * y_t.T


def _dsa_attn_call(x, mod, layer, j, qlt, qidxt, widxt, ckv, ckvt, kidx, wuvt, wot):
    b, t, d = x.shape
    tq = ATTN_Q_TILE
    ck = ATTN_KEY_CHUNK
    nc = t // ck
    top_k = min(TOPK_MAX, t // 4)
    seq_bits = (t - 1).bit_length()
    kernel = functools.partial(_dsa_attn_kernel, top_k=top_k, seq_bits=seq_bits)
    return pl.pallas_call(
        kernel,
        grid=(b, t // tq),
        in_specs=[pl.BlockSpec((1, tq, d), lambda i, k: (i, k, 0)),
                  _mod_spec(layer, 2, d),
                  pl.BlockSpec((1, N_HEADS, KV_RANK, tq), lambda i, k: (i, 0, 0, k)),
                  pl.BlockSpec((1, _QIDX_ROWS, tq), lambda i, k: (i, 0, k)),
                  pl.BlockSpec((1, IDX_HEADS, tq), lambda i, k: (i, 0, k)),
                  pl.BlockSpec((1, nc, ck, KV_RANK), lambda i, k: (i, 0, 0, 0)),
                  pl.BlockSpec((1, nc, KV_RANK + ONES_ROWS, ck), lambda i, k: (i, 0, 0, 0)),
                  pl.BlockSpec((1, nc, ck, IDX_HEAD_DIM), lambda i, k: (i, 0, 0, 0)),
                  _const_spec((None, N_HEADS, V_HEAD_DIM, KV_RANK), (j, 0, 0, 0)),
                  _const_spec((None, d, N_HEADS * V_HEAD_DIM), (j, 0, 0))],
        out_specs=pl.BlockSpec((1, tq, d), lambda i, k: (i, k, 0)),
        out_shape=jax.ShapeDtypeStruct((b, t, d), F32),
        scratch_shapes=[pltpu.VMEM((nc, ck, tq), I32),
                        pltpu.VMEM((nc, ck, tq), I16),
                        pltpu.VMEM((N_HEADS, KV_RANK + ONES_ROWS, tq), F32),
                        pltpu.VMEM((N_HEADS, tq), F32),
                        pltpu.VMEM((N_HEADS * V_HEAD_DIM, tq), BF16)],
        compiler_params=_params(2),
        name="dsa_attn",
    )(x, mod, qlt, qidxt, widxt,
      ckv.reshape(b, nc, ck, KV_RANK), ckvt, kidx.reshape(b, nc, ck, IDX_HEAD_DIM), wuvt, wot)


def _prep_dsa_weights(a_w_in, a_w_uk, a_w_uv, a_w_o, a_g_kidx, a_b_kidx):
    n_a, d, _ = a_w_in.shape
    o1 = N_HEADS * QK_HEAD_DIM
    o2 = o1 + KV_RANK
    o3 = o2 + _QIDX_ROWS
    o4 = o3 + IDX_HEAD_DIM
    pad_cols = jnp.zeros((n_a, d, 128 - IDX_HEAD_DIM), a_w_in.dtype)
    wa = jnp.concatenate([a_w_in[..., :o2], a_w_in[..., o3:o4], pad_cols], axis=-1).astype(BF16)
    pad_rows = jnp.zeros((n_a, _WB_ROWS - _QIDX_ROWS - IDX_HEADS, d), a_w_in.dtype)
    wbt = jnp.concatenate([jnp.swapaxes(a_w_in[..., o2:o3], 1, 2),
                           jnp.swapaxes(a_w_in[..., o4:], 1, 2), pad_rows], axis=1).astype(BF16)
    wukt = jnp.swapaxes(a_w_uk, 2, 3).astype(BF16)
    wuvt = jnp.swapaxes(a_w_uv, 2, 3).astype(BF16)
    wot = jnp.swapaxes(a_w_o, 1, 2).astype(BF16)
    pad = jnp.zeros((n_a, 128 - IDX_HEAD_DIM), F32)
    gk = jnp.concatenate([a_g_kidx, pad], axis=-1)
    bk = jnp.concatenate([a_b_kidx, pad], axis=-1)
    return wa, wbt, wukt, wuvt, wot, gk[:, None, :], bk[:, None, :]


def kernel(x, c, mod_w, mod_b, norm_mix_g, norm_ffn_g, a_w_in, a_g_kv, a_g_kidx, a_b_kidx, a_w_uk, a_w_uv,
           a_w_o, b_w_in, b_ln_g, b_ln_b, b_w_s, b_b_s, b_w_out, ffn_w_gate, ffn_w_up, ffn_w_down, final_g):
    depth = mod_w.shape[0]
    b, t, d = x.shape
    assert t % ATTN_KEY_CHUNK == 0 and ATTN_KEY_CHUNK % TOKEN_TILE == 0 and t % ATTN_Q_TILE == 0
    assert t % FFN_TOKEN_TILE == 0

    mod = _mod_call(c, mod_w, mod_b)
    wa, wbt, wukt, wuvt, wot, gk, bk = _prep_dsa_weights(a_w_in, a_w_uk, a_w_uv, a_w_o, a_g_kidx, a_b_kidx)
    b_w_in16 = b_w_in.astype(BF16)
    b_w_out16 = b_w_out.astype(BF16)
    b_bs = b_b_s[..., None]
    wg16 = ffn_w_gate.astype(BF16)
    wu16 = ffn_w_up.astype(BF16)
    wd16 = ffn_w_down.astype(BF16)
    final_g2 = final_g.reshape(1, d)

    for layer in range(depth):
        j = layer // 2
        g_mix = norm_mix_g[layer].reshape(1, d)
        if layer % 2 == 0:
            qlt, ckv, ckvt, kidx, qidxt, widxt = _dsa_in_call(
                x, mod, layer, g_mix, j, wa, wbt, wukt, a_g_kv[:, None, :], gk, bk)
            x = _dsa_attn_call(x, mod, layer, j, qlt, qidxt, widxt, ckv, ckvt, kidx, wuvt, wot)
        else:
            x = _gmlp_call(x, mod, layer, g_mix, j, b_w_in16, b_ln_g[:, None, :], b_ln_b[:, None, :],
                           b_w_s, b_bs, b_w_out16)
        x = _ffn_call(x, mod, layer, norm_ffn_g[layer].reshape(1, d), wg16, wu16, wd16,
                      final_g2 if layer == depth - 1 else None)
    return x
```

```python
import functools

import jax
import jax.numpy as jnp
from jax import lax
from jax.experimental import pallas as pl
from jax.experimental.pallas import tpu as pltpu

F32 = jnp.float32
BF16 = jnp.bfloat16
I32 = jnp.int32
I16 = jnp.int16

EPS = 1e-6
N_MOD = 6
N_HEADS = 8
QK_HEAD_DIM = 128
V_HEAD_DIM = 128
KV_RANK = 256
IDX_HEADS = 8
IDX_HEAD_DIM = 64
TOPK_MAX = 256
GMLP_GROUPS = 8
CHUNK = 128

TOKEN_TILE = 256
FFN_TOKEN_TILE = 512
ATTN_Q_TILE = 256
ATTN_KEY_CHUNK = 512
VMEM_LIMIT_BYTES = 56 * 1024 * 1024

INT_MIN = -(2 ** 31)
I16_MIN = -(2 ** 15)
LOG2_E = 1.4426950408889634
ONES_ROWS = 16
NEG_BIG = -1e30

_NT_DIMS = (((1,), (1,)), ((), ()))


def _dot(a, b):
    return jnp.dot(a, b, preferred_element_type=F32)


def _dot_nt(a, b):
    return lax.dot_general(a, b, _NT_DIMS, preferred_element_type=F32)


def _norm_mod(x, g, shift, scale):
    ms = jnp.mean(x * x, axis=-1, keepdims=True)
    y = x * lax.rsqrt(ms + EPS) * g
    return y * (1.0 + scale) + shift


def _const_spec(block_shape, index):
    return pl.BlockSpec(block_shape, lambda *_: index, pipeline_mode=pl.Buffered(1))


def _params(n_axes):
    return pltpu.CompilerParams(dimension_semantics=("arbitrary",) * n_axes,
                                vmem_limit_bytes=VMEM_LIMIT_BYTES)


def _mod_kernel(c_ref, w_ref, b_ref, o_ref):
    c = c_ref[...]
    c_act = c * jax.nn.sigmoid(c)
    o_ref[0] = jnp.dot(c_act, w_ref[0], preferred_element_type=F32,
                       precision=lax.Precision.HIGHEST) + b_ref[0]


def _mod_call(c, mod_w, mod_b):
    depth, d, n = mod_w.shape
    b = c.shape[0]
    tn = 1024
    out = pl.pallas_call(
        _mod_kernel,
        grid=(depth, n // tn),
        in_specs=[pl.BlockSpec((b, d), lambda l, j: (0, 0)),
                  pl.BlockSpec((1, d, tn), lambda l, j: (l, 0, j)),
                  pl.BlockSpec((1, 1, tn), lambda l, j: (l, 0, j))],
        out_specs=pl.BlockSpec((1, b, tn), lambda l, j: (l, 0, j)),
        out_shape=jax.ShapeDtypeStruct((depth, b, n), F32),
        compiler_params=_params(2),
        name="mod",
    )(c, mod_w, mod_b.reshape(depth, 1, n))
    return out.reshape(depth, b, 1, n)


def _mod_spec(layer, which, d):
    return pl.BlockSpec((1, 1, 1, d), lambda b, t: (layer, b, 0, which))


def _ffn_kernel(x_ref, g_ref, sh_ref, sc_ref, gt_ref, wg_ref, wu_ref, wd_ref, *rest, final):
    o_ref = rest[-1]
    x = x_ref[0]
    h = _norm_mod(x, g_ref[...], sh_ref[0, 0], sc_ref[0, 0]).astype(BF16)
    gate = _dot(h, wg_ref[...])
    up = _dot(h, wu_ref[...])
    act = (gate * jax.nn.sigmoid(gate) * up).astype(BF16)
    y = _dot(act, wd_ref[...])
    xn = x + gt_ref[0, 0] * y
    if final:
        fg_ref = rest[0]
        ms = jnp.mean(xn * xn, axis=-1, keepdims=True)
        xn = xn * lax.rsqrt(ms + EPS) * fg_ref[...]
    o_ref[0] = xn


def _ffn_call(x, mod, layer, norm_g, wg, wu, wd, final_g):
    b, t, d = x.shape
    dff = wg.shape[-1]
    tm = FFN_TOKEN_TILE
    final = final_g is not None
    in_specs = [pl.BlockSpec((1, tm, d), lambda i, j: (i, j, 0)),
                _const_spec((1, d), (0, 0)),
                _mod_spec(layer, 3, d), _mod_spec(layer, 4, d), _mod_spec(layer, 5, d),
                _const_spec((None, d, dff), (layer, 0, 0)),
                _const_spec((None, d, dff), (layer, 0, 0)),
                _const_spec((None, dff, d), (layer, 0, 0))]
    args = [x, norm_g, mod, mod, mod, wg, wu, wd]
    if final:
        in_specs.append(_const_spec((1, d), (0, 0)))
        args.append(final_g)
    return pl.pallas_call(
        functools.partial(_ffn_kernel, final=final),
        grid=(b, t // tm),
        in_specs=in_specs,
        out_specs=pl.BlockSpec((1, tm, d), lambda i, j: (i, j, 0)),
        out_shape=jax.ShapeDtypeStruct((b, t, d), F32),
        compiler_params=_params(2),
        name="ffn",
    )(*args)


def _gmlp_kernel(x_ref, g_ref, sh_ref, sc_ref, gt_ref, win_ref, lng_ref, lnb_ref, ws_ref, bs_ref,
                 wout_ref, o_ref, y_scr):
    tm = x_ref.shape[1]
    half = lng_ref.shape[-1]
    gdim = half // GMLP_GROUPS
    x = x_ref[0]
    h = _norm_mod(x, g_ref[...], sh_ref[0, 0], sc_ref[0, 0]).astype(BF16)
    z = _dot(h, win_ref[...])
    z = 0.5 * z * (1.0 + lax.erf(z * (2.0 ** -0.5)))
    v = z[:, half:]
    mu = jnp.mean(v, axis=-1, keepdims=True)
    vc = v - mu
    var = jnp.mean(vc * vc, axis=-1, keepdims=True)
    vn = (vc * lax.rsqrt(var + EPS) * lng_ref[...] + lnb_ref[...]).astype(BF16)
    row = lax.broadcasted_iota(I32, (CHUNK, CHUNK), 0)
    col = lax.broadcasted_iota(I32, (CHUNK, CHUNK), 1)
    tril = col <= row
    for g in range(GMLP_GROUPS):
        w_causal = jnp.where(tril, ws_ref[g], 0.0).astype(BF16)
        bias = bs_ref[g]
        lo = g * gdim
        for c in range(tm // CHUNK):
            r0 = c * CHUNK
            mixed = _dot(w_causal, vn[r0:r0 + CHUNK, lo:lo + gdim]) + bias
            y_scr[r0:r0 + CHUNK, lo:lo + gdim] = (z[r0:r0 + CHUNK, lo:lo + gdim] * mixed).astype(BF16)
    y = _dot(y_scr[...], wout_ref[...])
    o_ref[0] = x + gt_ref[0, 0] * y


def _gmlp_call(x, mod, layer, norm_g, j, w_in, ln_g, ln_b, w_s, b_s, w_out):
    b, t, d = x.shape
    half = ln_g.shape[-1]
    tm = FFN_TOKEN_TILE
    return pl.pallas_call(
        _gmlp_kernel,
        grid=(b, t // tm),
        in_specs=[pl.BlockSpec((1, tm, d), lambda i, k: (i, k, 0)),
                  _const_spec((1, d), (0, 0)),
                  _mod_spec(layer, 0, d), _mod_spec(layer, 1, d), _mod_spec(layer, 2, d),
                  _const_spec((None, d, 2 * half), (j, 0, 0)),
                  _const_spec((None, 1, half), (j, 0, 0)),
                  _const_spec((None, 1, half), (j, 0, 0)),
                  _const_spec((None, GMLP_GROUPS, CHUNK, CHUNK), (j, 0, 0, 0)),
                  _const_spec((None, GMLP_GROUPS, CHUNK, 1), (j, 0, 0, 0)),
                  _const_spec((None, half, d), (j, 0, 0))],
        out_specs=pl.BlockSpec((1, tm, d), lambda i, k: (i, k, 0)),
        out_shape=jax.ShapeDtypeStruct((b, t, d), F32),
        scratch_shapes=[pltpu.VMEM((tm, half), BF16)],
        compiler_params=_params(2),
        name="gmlp",
    )(x, norm_g, mod, mod, mod, w_in, ln_g, ln_b, w_s, b_s, w_out)


_O_CKV = N_HEADS * QK_HEAD_DIM
_O_KIDX = _O_CKV + KV_RANK
_WA_COLS = _O_KIDX + 128
_QIDX_ROWS = IDX_HEADS * IDX_HEAD_DIM
_WB_ROWS = _QIDX_ROWS + 16


def _dsa_in_kernel(x_ref, g_ref, sh_ref, sc_ref, wa_ref, wbt_ref, wukt_ref, gkv_ref, gk_ref, bk_ref,
                   qlt_ref, ckv_ref, ckvt_ref, kidx_ref, qidxt_ref, widxt_ref):
    x = x_ref[0]
    h = _norm_mod(x, g_ref[...], sh_ref[0, 0], sc_ref[0, 0]).astype(BF16)
    pa = _dot(h, wa_ref[...])
    for hh in range(N_HEADS):
        qh = pa[:, hh * QK_HEAD_DIM:(hh + 1) * QK_HEAD_DIM].astype(BF16)
        qlt = _dot_nt(wukt_ref[hh], qh) * (QK_HEAD_DIM ** -0.5 * LOG2_E)
        qlt_ref[0, hh] = qlt.astype(BF16)
    ck = pa[:, _O_CKV:_O_KIDX]
    ckn = ck * lax.rsqrt(jnp.mean(ck * ck, axis=-1, keepdims=True) + EPS) * gkv_ref[...]
    ckv_ref[0] = ckn.astype(BF16)
    ckvt_ref[0, 0, :KV_RANK, :] = ckn.T.astype(BF16)
    ckvt_ref[0, 0, KV_RANK:, :] = jnp.ones((ONES_ROWS, ckn.shape[0]), BF16)
    kr = pa[:, _O_KIDX:_O_KIDX + 128]
    lane = lax.broadcasted_iota(I32, kr.shape, 1)
    real = lane < IDX_HEAD_DIM
    mu = jnp.sum(jnp.where(real, kr, 0.0), axis=-1, keepdims=True) * (1.0 / IDX_HEAD_DIM)
    kc = jnp.where(real, kr - mu, 0.0)
    var = jnp.sum(kc * kc, axis=-1, keepdims=True) * (1.0 / IDX_HEAD_DIM)
    kn = kc * lax.rsqrt(var + EPS) * gk_ref[...] + bk_ref[...]
    kidx_ref[0] = kn[:, :IDX_HEAD_DIM].astype(BF16)
    pb = _dot_nt(wbt_ref[...], h)
    qidxt_ref[0] = pb[:_QIDX_ROWS].astype(BF16)
    widxt_ref[0] = pb[_QIDX_ROWS:_QIDX_ROWS + IDX_HEADS] * (IDX_HEADS ** -0.5 * IDX_HEAD_DIM ** -0.5)


def _dsa_in_call(x, mod, layer, norm_g, j, wa, wbt, wukt, g_kv, g_kidx, b_kidx):
    b, t, d = x.shape
    tm = TOKEN_TILE
    ck = ATTN_KEY_CHUNK
    r = ck // tm
    out_shape = [jax.ShapeDtypeStruct((b, N_HEADS, KV_RANK, t), BF16),
                 jax.ShapeDtypeStruct((b, t, KV_RANK), BF16),
                 jax.ShapeDtypeStruct((b, t // ck, KV_RANK + ONES_ROWS, ck), BF16),
                 jax.ShapeDtypeStruct((b, t, IDX_HEAD_DIM), BF16),
                 jax.ShapeDtypeStruct((b, _QIDX_ROWS, t), BF16),
                 jax.ShapeDtypeStruct((b, IDX_HEADS, t), F32)]
    out_specs = [pl.BlockSpec((1, N_HEADS, KV_RANK, tm), lambda i, k: (i, 0, 0, k)),
                 pl.BlockSpec((1, tm, KV_RANK), lambda i, k: (i, k, 0)),
                 pl.BlockSpec((1, 1, KV_RANK + ONES_ROWS, tm), lambda i, k: (i, k // r, 0, k % r)),
                 pl.BlockSpec((1, tm, IDX_HEAD_DIM), lambda i, k: (i, k, 0)),
                 pl.BlockSpec((1, _QIDX_ROWS, tm), lambda i, k: (i, 0, k)),
                 pl.BlockSpec((1, IDX_HEADS, tm), lambda i, k: (i, 0, k))]
    return pl.pallas_call(
        _dsa_in_kernel,
        grid=(b, t // tm),
        in_specs=[pl.BlockSpec((1, tm, d), lambda i, k: (i, k, 0)),
                  _const_spec((1, d), (0, 0)),
                  _mod_spec(layer, 0, d), _mod_spec(layer, 1, d),
                  _const_spec((None, d, _WA_COLS), (j, 0, 0)),
                  _const_spec((None, _WB_ROWS, d), (j, 0, 0)),
                  _const_spec((None, N_HEADS, KV_RANK, QK_HEAD_DIM), (j, 0, 0, 0)),
                  _const_spec((None, 1, KV_RANK), (j, 0, 0)),
                  _const_spec((None, 1, 128), (j, 0, 0)),
                  _const_spec((None, 1, 128), (j, 0, 0))],
        out_specs=out_specs,
        out_shape=out_shape,
        compiler_params=_params(2),
        name="dsa_in",
    )(x, norm_g, mod, mod, wa, wbt, wukt, g_kv, g_kidx, b_kidx)


def _dsa_attn_kernel(x_ref, gt_ref, qlt_ref, qidxt_ref, widxt_ref, ckv_ref, ckvt_ref, kidx_ref,
                     wuvt_ref, wot_ref, o_ref, keys_scr, half_scr, acc_scr, m_scr, ot_scr, *, top_k, seq_bits):
    tq = x_ref.shape[1]
    ck = ckv_ref.shape[2]
    q0 = pl.program_id(1) * tq
    nk = (q0 + tq + ck - 1) // ck
    qpos = q0 + lax.broadcasted_iota(I32, (1, tq), 1)
    kofs = lax.broadcasted_iota(I32, (ck, 1), 0)
    w_t = widxt_ref[0]

    def score_body(c, carry):
        kc = kidx_ref[0, c]
        s = jnp.zeros((ck, tq), F32)
        for hh in range(IDX_HEADS):
            lg = _dot(kc, qidxt_ref[0, hh * IDX_HEAD_DIM:(hh + 1) * IDX_HEAD_DIM, :])
            s = s + jnp.maximum(lg, 0.0) * w_t[hh:hh + 1, :]
        bits = pltpu.bitcast(s, I32)
        key = jnp.where(bits < 0, bits ^ jnp.int32(0x7FFFFFFF), bits)
        causal = (c * ck + kofs) <= qpos
        key = jnp.where(causal, key, jnp.int32(INT_MIN))
        keys_scr[c] = key
        half_scr[c] = jnp.right_shift(key, 16).astype(I16)
        return carry

    lax.fori_loop(0, nk, score_body, 0)

    def count(src_scr, one, pred):
        def body(c, acc):
            m = pred(src_scr[c], c)
            for r in range(ck // 64):
                acc = jnp.where(m[r * 64:(r + 1) * 64], acc + one, acc)
            return acc
        acc = lax.fori_loop(0, nk, body, jnp.zeros((64, tq), one.dtype))
        return jnp.sum(acc.astype(I32), axis=0, keepdims=True)

    def count_ge(cand):
        return count(keys_scr, jnp.int32(1), lambda k, c: k >= cand)

    def count16(pred):
        return count(half_scr, jnp.int16(1), pred)

    def search16(target, early_exit=False):
        def step(i, t, n_t):
            cand = t + jnp.left_shift(jnp.int32(1), 15 - i)
            c16 = cand.astype(I16)
            n = count16(lambda v, c: v >= c16)
            take = n >= target
            return jnp.where(take, cand, t), jnp.where(take, n, n_t)

        t0 = jnp.full((1, tq), I16_MIN, I32)
        if not early_exit:
            return lax.fori_loop(0, 16, lambda i, t: step(i, t, t)[0], t0)

        def body(state):
            i, _, t, n_t = state
            t, n_t = step(i, t, n_t)
            unsettled = jnp.max(jnp.where(n_t == target, 0, 1))
            return i + 1, unsettled, t, n_t

        state = (jnp.int32(0), jnp.int32(1), t0, jnp.full((1, tq), -1, I32))
        return lax.while_loop(lambda st: jnp.logical_and(st[0] < 16, st[1] > 0), body, state)[2]

    t_hi = search16(top_k)
    t_hi16 = t_hi.astype(I16)
    n_above = count16(lambda v, c: v > t_hi16)

    def low_half_body(c, carry):
        k = keys_scr[c]
        match = jnp.right_shift(k, 16) == t_hi
        low = jnp.bitwise_and(k, 0xFFFF) + I16_MIN
        half_scr[c] = jnp.where(match, low, I16_MIN).astype(I16)
        return carry

    lax.fori_loop(0, nk, low_half_body, 0)
    t_lo = search16(top_k - n_above, early_exit=True)
    thr = jnp.left_shift(t_hi, 16) + (t_lo - I16_MIN)
    thr = jnp.maximum(thr, jnp.int32(INT_MIN + 1))

    n_ge = count_ge(thr)
    has_tie = jnp.max(jnp.where(n_ge > top_k, 1, 0)) > 0

    @pl.when(has_tie)
    def _():
        need = top_k - count_ge(thr + 1)

        def count_eq_before(p):
            return count(keys_scr, jnp.int32(1), lambda k, c: (k == thr) & ((c * ck + kofs) < p))

        def idx_body(i, p):
            cand = p + jnp.left_shift(jnp.int32(1), seq_bits - 1 - i)
            return jnp.where(count_eq_before(cand) < need, cand, p)

        last = lax.fori_loop(0, seq_bits, idx_body, jnp.zeros((1, tq), I32))

        def demote(c, carry):
            k = keys_scr[c]
            drop = (k == thr) & ((c * ck + kofs) > last)
            keys_scr[c] = jnp.where(drop, jnp.int32(INT_MIN), k)
            return carry

        lax.fori_loop(0, nk, demote, 0)

    m_scr[...] = jnp.full(m_scr.shape, NEG_BIG, F32)
    acc_scr[...] = jnp.zeros(acc_scr.shape, F32)

    def attn_body(c, carry):
        bias = jnp.where(keys_scr[c] >= thr, 0.0, NEG_BIG)
        kc = ckv_ref[0, c]
        kct = ckvt_ref[0, c]
        scores = [_dot(kc, qlt_ref[0, hh]) + bias for hh in range(N_HEADS)]
        for hh in range(N_HEADS):
            s = scores[hh]
            m_old = m_scr[hh:hh + 1, :]
            m_new = jnp.maximum(m_old, jnp.max(s, axis=0, keepdims=True))
            p = jnp.exp2(s - m_new)
            alpha = jnp.exp2(m_old - m_new)
            acc_scr[hh] = alpha * acc_scr[hh] + _dot(kct, p.astype(BF16))
            m_scr[hh:hh + 1, :] = m_new
        return carry

    lax.fori_loop(0, nk, attn_body, 0)

    for hh in range(N_HEADS):
        inv_l = 1.0 / acc_scr[hh, KV_RANK:KV_RANK + 1, :]
        o_lat_t = (acc_scr[hh, :KV_RANK, :] * inv_l).astype(BF16)
        ot_scr[hh * V_HEAD_DIM:(hh + 1) * V_HEAD_DIM, :] = _dot(wuvt_ref[hh], o_lat_t).astype(BF16)
    y_t = _dot(wot_ref[...], ot_scr[...])
    o_ref[0] = x_ref[0] + gt_ref[0, 0] * y_t.T


def _dsa_attn_call(x, mod, layer, j, qlt, qidxt, widxt, ckv, ckvt, kidx, wuvt, wot):
    b, t, d = x.shape
    tq = ATTN_Q_TILE
    ck = ATTN_KEY_CHUNK
    nc = t // ck
    top_k = min(TOPK_MAX, t // 4)
    seq_bits = (t - 1).bit_length()
    kernel = functools.partial(_dsa_attn_kernel, top_k=top_k, seq_bits=seq_bits)
    return pl.pallas_call(
        kernel,
        grid=(b, t // tq),
        in_specs=[pl.BlockSpec((1, tq, d), lambda i, k: (i, k, 0)),
                  _mod_spec(layer, 2, d),
                  pl.BlockSpec((1, N_HEADS, KV_RANK, tq), lambda i, k: (i, 0, 0, k)),
                  pl.BlockSpec((1, _QIDX_ROWS, tq), lambda i, k: (i, 0, k)),
                  pl.BlockSpec((1, IDX_HEADS, tq), lambda i, k: (i, 0, k)),
                  pl.BlockSpec((1, nc, ck, KV_RANK), lambda i, k: (i, 0, 0, 0)),
                  pl.BlockSpec((1, nc, KV_RANK + ONES_ROWS, ck), lambda i, k: (i, 0, 0, 0)),
                  pl.BlockSpec((1, nc, ck, IDX_HEAD_DIM), lambda i, k: (i, 0, 0, 0)),
                  _const_spec((None, N_HEADS, V_HEAD_DIM, KV_RANK), (j, 0, 0, 0)),
                  _const_spec((None, d, N_HEADS * V_HEAD_DIM), (j, 0, 0))],
        out_specs=pl.BlockSpec((1, tq, d), lambda i, k: (i, k, 0)),
        out_shape=jax.ShapeDtypeStruct((b, t, d), F32),
        scratch_shapes=[pltpu.VMEM((nc, ck, tq), I32),
                        pltpu.VMEM((nc, ck, tq), I16),
                        pltpu.VMEM((N_HEADS, KV_RANK + ONES_ROWS, tq), F32),
                        pltpu.VMEM((N_HEADS, tq), F32),
                        pltpu.VMEM((N_HEADS * V_HEAD_DIM, tq), BF16)],
        compiler_params=_params(2),
        name="dsa_attn",
    )(x, mod, qlt, qidxt, widxt,
      ckv.reshape(b, nc, ck, KV_RANK), ckvt, kidx.reshape(b, nc, ck, IDX_HEAD_DIM), wuvt, wot)


def _prep_dsa_weights(a_w_in, a_w_uk, a_w_uv, a_w_o, a_g_kidx, a_b_kidx):
    n_a, d, _ = a_w_in.shape
    o1 = N_HEADS * QK_HEAD_DIM
    o2 = o1 + KV_RANK
    o3 = o2 + _QIDX_ROWS
    o4 = o3 + IDX_HEAD_DIM
    pad_cols = jnp.zeros((n_a, d, 128 - IDX_HEAD_DIM), a_w_in.dtype)
    wa = jnp.concatenate([a_w_in[..., :o2], a_w_in[..., o3:o4], pad_cols], axis=-1).astype(BF16)
    pad_rows = jnp.zeros((n_a, _WB_ROWS - _QIDX_ROWS - IDX_HEADS, d), a_w_in.dtype)
    wbt = jnp.concatenate([jnp.swapaxes(a_w_in[..., o2:o3], 1, 2),
                           jnp.swapaxes(a_w_in[..., o4:], 1, 2), pad_rows], axis=1).astype(BF16)
    wukt = jnp.swapaxes(a_w_uk, 2, 3).astype(BF16)
    wuvt = jnp.swapaxes(a_w_uv, 2, 3).astype(BF16)
    wot = jnp.swapaxes(a_w_o, 1, 2).astype(BF16)
    pad = jnp.zeros((n_a, 128 - IDX_HEAD_DIM), F32)
    gk = jnp.concatenate([a_g_kidx, pad], axis=-1)
    bk = jnp.concatenate([a_b_kidx, pad], axis=-1)
    return wa, wbt, wukt, wuvt, wot, gk[:, None, :], bk[:, None, :]


def kernel(x, c, mod_w, mod_b, norm_mix_g, norm_ffn_g, a_w_in, a_g_kv, a_g_kidx, a_b_kidx, a_w_uk, a_w_uv,
           a_w_o, b_w_in, b_ln_g, b_ln_b, b_w_s, b_b_s, b_w_out, ffn_w_gate, ffn_w_up, ffn_w_down, final_g):
    depth = mod_w.shape[0]
    b, t, d = x.shape
    assert t % ATTN_KEY_CHUNK == 0 and ATTN_KEY_CHUNK % TOKEN_TILE == 0 and t % ATTN_Q_TILE == 0
    assert t % FFN_TOKEN_TILE == 0

    mod = _mod_call(c, mod_w, mod_b)
    wa, wbt, wukt, wuvt, wot, gk, bk = _prep_dsa_weights(a_w_in, a_w_uk, a_w_uv, a_w_o, a_g_kidx, a_b_kidx)
    b_w_in16 = b_w_in.astype(BF16)
    b_w_out16 = b_w_out.astype(BF16)
    b_bs = b_b_s[..., None]
    wg16 = ffn_w_gate.astype(BF16)
    wu16 = ffn_w_up.astype(BF16)
    wd16 = ffn_w_down.astype(BF16)
    final_g2 = final_g.reshape(1, d)

    for layer in range(depth):
        j = layer // 2
        g_mix = norm_mix_g[layer].reshape(1, d)
        if layer % 2 == 0:
            qlt, ckv, ckvt, kidx, qidxt, widxt = _dsa_in_call(
                x, mod, layer, g_mix, j, wa, wbt, wukt, a_g_kv[:, None, :], gk, bk)
            x = _dsa_attn_call(x, mod, layer, j, qlt, qidxt, widxt, ckv, ckvt, kidx, wuvt, wot)
        else:
            x = _gmlp_call(x, mod, layer, g_mix, j, b_w_in16, b_ln_g[:, None, :], b_ln_b[:, None, :],
                           b_w_s, b_bs, b_w_out16)
        x = _ffn_call(x, mod, layer, norm_ffn_g[layer].reshape(1, d), wg16, wu16, wd16,
                      final_g2 if layer == depth - 1 else None)
    return x
```
